```python
import math
import jax, jax.numpy as jnp
from jax import lax
import numpy as np

D_MODEL = 2048
BATCH = 2
SEQ = 8192
DEPTH = 4

CHUNK = 64
N_A_LAYERS = DEPTH // 2
N_B_LAYERS = DEPTH - N_A_LAYERS
D_FF = 4 * D_MODEL

SSM_EXPAND = 2
D_INNER = SSM_EXPAND * D_MODEL
SSM_HEAD_DIM = 64
SSM_HEADS = D_INNER // SSM_HEAD_DIM
SSM_GROUPS = 8
SSM_STATE = 128
SSM_CONV = 4
CONV_DIM = D_INNER + 2 * SSM_GROUPS * SSM_STATE
IN_PROJ_DIM = 2 * D_INNER + 2 * SSM_GROUPS * SSM_STATE + SSM_HEADS

DIFF_HEAD_DIM = 128
DIFF_HEADS = D_MODEL // (2 * DIFF_HEAD_DIM)
DIFF_V_DIM = 2 * DIFF_HEAD_DIM
Q_BLOCK = 128
EPS = 1e-5

kernel_name = "yoco_mamba2_diff_attn_trunk"


def rms_norm(x, w):
    xf = x.astype(jnp.float32)
    xf = xf * lax.rsqrt(jnp.mean(xf * xf, axis=-1, keepdims=True) + EPS)
    return (xf * w.astype(jnp.float32)).astype(x.dtype)


def gated_group_rms_norm(y, z, w):
    g = (y * jax.nn.silu(z)).astype(jnp.float32)
    shp = g.shape
    g = g.reshape(shp[:-1] + (SSM_GROUPS, D_INNER // SSM_GROUPS))
    g = g * lax.rsqrt(jnp.mean(g * g, axis=-1, keepdims=True) + EPS)
    return (g.reshape(shp) * w.astype(jnp.float32)).astype(y.dtype)


def causal_depthwise_conv(x, w, b):
    out = lax.conv_general_dilated(
        x, w[:, None, :].astype(x.dtype), window_strides=(1,), padding=[(SSM_CONV - 1, 0)],
        dimension_numbers=("NWC", "WIO", "NWC"), feature_group_count=x.shape[-1])
    return out + b.astype(x.dtype)


def ssd_scan(x, da, bm, cm):
    out_dtype = x.dtype
    b, s, h, p = x.shape
    g, n = bm.shape[2], bm.shape[3]
    r = h // g
    nc = s // CHUNK
    xc = x.astype(jnp.float32).reshape(b, nc, CHUNK, g, r, p)
    ac = da.astype(jnp.float32).reshape(b, nc, CHUNK, g, r)
    bc = bm.astype(jnp.float32).reshape(b, nc, CHUNK, g, n)
    cc = cm.astype(jnp.float32).reshape(b, nc, CHUNK, g, n)
    a_cs = jnp.cumsum(ac, axis=2)
    seg = a_cs[:, :, :, None] - a_cs[:, :, None, :]
    tri = jnp.tril(jnp.ones((CHUNK, CHUNK), dtype=bool))[None, None, :, :, None, None]
    decay = jnp.exp(jnp.where(tri, seg, -jnp.inf))
    cb = jnp.einsum('bclgn,bcsgn->bclsg', cc, bc)
    y_diag = jnp.einsum('bclsg,bclsgr,bcsgrp->bclgrp', cb, decay, xc)
    decay_to_end = jnp.exp(a_cs[:, :, -1:] - a_cs)
    states = jnp.einsum('bclgn,bclgr,bclgrp->bcgrpn', bc, decay_to_end, xc)
    chunk_decay = jnp.exp(a_cs[:, :, -1])

    def step(carry, inp):
        st, dec = inp
        new = carry * dec[..., None, None] + st
        return new, carry

    init = jnp.zeros((b, g, r, p, n), jnp.float32)
    _, prev = lax.scan(step, init, (jnp.moveaxis(states, 1, 0), jnp.moveaxis(chunk_decay, 1, 0)))
    prev = jnp.moveaxis(prev, 0, 1)
    y_off = jnp.einsum('bclgn,bcgrpn,bclgr->bclgrp', cc, prev, jnp.exp(a_cs))
    return (y_diag + y_off).reshape(b, s, h, p).astype(out_dtype)


def mamba2_mixer(h, w_in, conv_w, conv_b, dt_bias, a_log, d_skip, out_norm_w, w_out):
    b, s, _ = h.shape
    zxbcdt = h @ w_in
    z = zxbcdt[..., :D_INNER]
    xbc = zxbcdt[..., D_INNER:D_INNER + CONV_DIM]
    dt = zxbcdt[..., D_INNER + CONV_DIM:]
    xbc = jax.nn.silu(causal_depthwise_conv(xbc, conv_w, conv_b))
    xs = xbc[..., :D_INNER].reshape(b, s, SSM_HEADS, SSM_HEAD_DIM)
    bm = xbc[..., D_INNER:D_INNER + SSM_GROUPS * SSM_STATE].reshape(b, s, SSM_GROUPS, SSM_STATE)
    cm = xbc[..., D_INNER + SSM_GROUPS * SSM_STATE:].reshape(b, s, SSM_GROUPS, SSM_STATE)
    dt = jax.nn.softplus((dt + dt_bias).astype(jnp.float32))
    a = -jnp.exp(a_log.astype(jnp.float32))
    y = ssd_scan(xs * dt[..., None].astype(xs.dtype), dt * a, bm, cm)
    y = y + xs * d_skip[:, None].astype(xs.dtype)
    y = gated_group_rms_norm(y.reshape(b, s, D_INNER), z, out_norm_w)
    return y @ w_out


def shared_kv(h, kv_norm_w, w_kv, k_norm_w):
    b, s, _ = h.shape
    kv = rms_norm(h, kv_norm_w) @ w_kv
    k = rms_norm(kv[..., :D_MODEL].reshape(b, s, DIFF_HEADS, 2, DIFF_HEAD_DIM), k_norm_w)
    v = kv[..., D_MODEL:].reshape(b, s, DIFF_HEADS, DIFF_V_DIM)
    return k, v


def diff_attention(h, k, v, w_q, q_norm_w, lam_q1, lam_k1, lam_q2, lam_k2, subln_w, w_o, lambda_init):
    b, s, _ = h.shape
    q = rms_norm((h @ w_q).reshape(b, s, DIFF_HEADS, 2, DIFF_HEAD_DIM), q_norm_w)
    lam = (jnp.exp(jnp.sum(lam_q1.astype(jnp.float32) * lam_k1.astype(jnp.float32)))
           - jnp.exp(jnp.sum(lam_q2.astype(jnp.float32) * lam_k2.astype(jnp.float32)))
           + lambda_init)
    scale = DIFF_HEAD_DIM ** -0.5
    nqb = s // Q_BLOCK
    qb = jnp.moveaxis(q.reshape(b, nqb, Q_BLOCK, DIFF_HEADS, 2, DIFF_HEAD_DIM), 1, 0)
    key_chunk = jnp.arange(s) // CHUNK

    def block(args):
        qblk, i = args
        sc = jnp.einsum('bqhcd,bkhcd->bhcqk', qblk, k).astype(jnp.float32) * scale
        q_chunk = (i * Q_BLOCK + jnp.arange(Q_BLOCK)) // CHUNK
        mask = key_chunk[None, :] <= q_chunk[:, None]
        sc = jnp.where(mask[None, None, None], sc, -jnp.inf)
        pr = jax.nn.softmax(sc, axis=-1).astype(v.dtype)
        o = jnp.einsum('bhcqk,bkhv->bhcqv', pr, v)
        o = o[:, :, 0].astype(jnp.float32) - lam * o[:, :, 1].astype(jnp.float32)
        return jnp.transpose(o, (0, 2, 1, 3)).astype(h.dtype)

    o = lax.map(block, (qb, jnp.arange(nqb)))
    o = jnp.moveaxis(o, 0, 1).reshape(b, s, DIFF_HEADS, DIFF_V_DIM)
    o = rms_norm(o, subln_w) * (1.0 - lambda_init)
    return o.reshape(b, s, D_MODEL) @ w_o


def sq_relu_mlp(h, w_up, w_down):
    return jnp.square(jax.nn.relu(h @ w_up)) @ w_down


def setup_inputs(seed: int = 0) -> dict:
    key = jax.random.key(seed)
    ks = jax.random.split(key, 26)
    f32 = jnp.float32

    def nrm(k, shape, scale):
        return jax.random.normal(k, shape, f32) * scale

    def gain(k, shape):
        return 1.0 + 0.02 * jax.random.normal(k, shape, f32)

    u = jax.random.uniform(ks[8], (N_A_LAYERS, SSM_HEADS), f32)
    dt0 = jnp.maximum(jnp.exp(u * (math.log(0.1) - math.log(0.001)) + math.log(0.001)), 1e-4)
    dt_bias = dt0 + jnp.log(-jnp.expm1(-dt0))
    a_log = jnp.log(jax.random.uniform(ks[9], (N_A_LAYERS, SSM_HEADS), f32, 1.0, 16.0))
    return {
        "x": jax.random.normal(ks[0], (BATCH, SEQ, D_MODEL), f32),
        "mlp_norm_w": gain(ks[1], (DEPTH, D_MODEL)),
        "w_up": nrm(ks[2], (DEPTH, D_MODEL, D_FF), D_MODEL ** -0.5),
        "w_down": nrm(ks[3], (DEPTH, D_FF, D_MODEL), D_FF ** -0.5),
        "ssm_norm_w": gain(ks[4], (N_A_LAYERS, D_MODEL)),
        "ssm_w_in": nrm(ks[5], (N_A_LAYERS, D_MODEL, IN_PROJ_DIM), D_MODEL ** -0.5),
        "ssm_conv_w": nrm(ks[6], (N_A_LAYERS, SSM_CONV, CONV_DIM), SSM_CONV ** -0.5),
        "ssm_conv_b": nrm(ks[7], (N_A_LAYERS, CONV_DIM), 0.02),
        "ssm_dt_bias": dt_bias,
        "ssm_a_log": a_log,
        "ssm_d": gain(ks[10], (N_A_LAYERS, SSM_HEADS)),
        "ssm_out_norm_w": gain(ks[11], (N_A_LAYERS, D_INNER)),
        "ssm_w_out": nrm(ks[12], (N_A_LAYERS, D_INNER, D_MODEL), D_INNER ** -0.5),
        "kv_norm_w": gain(ks[13], (D_MODEL,)),
        "w_kv": nrm(ks[14], (D_MODEL, 2 * D_MODEL), D_MODEL ** -0.5),
        "k_norm_w": gain(ks[15], (DIFF_HEAD_DIM,)),
        "attn_norm_w": gain(ks[16], (N_B_LAYERS, D_MODEL)),
        "w_q": nrm(ks[17], (N_B_LAYERS, D_MODEL, D_MODEL), D_MODEL ** -0.5),
        "q_norm_w": gain(ks[18], (N_B_LAYERS, DIFF_HEAD_DIM)),
        "lam_q1": nrm(ks[19], (N_B_LAYERS, DIFF_HEAD_DIM), 0.1),
        "lam_k1": nrm(ks[20], (N_B_LAYERS, DIFF_HEAD_DIM), 0.1),
        "lam_q2": nrm(ks[21], (N_B_LAYERS, DIFF_HEAD_DIM), 0.1),
        "lam_k2": nrm(ks[22], (N_B_LAYERS, DIFF_HEAD_DIM), 0.1),
        "subln_w": gain(ks[23], (N_B_LAYERS, DIFF_V_DIM)),
        "w_o": nrm(ks[24], (N_B_LAYERS, D_MODEL, D_MODEL), D_MODEL ** -0.5),
    }


def reference(x, mlp_norm_w, w_up, w_down, ssm_norm_w, ssm_w_in, ssm_conv_w, ssm_conv_b,
              ssm_dt_bias, ssm_a_log, ssm_d, ssm_out_norm_w, ssm_w_out, kv_norm_w, w_kv,
              k_norm_w, attn_norm_w, w_q, q_norm_w, lam_q1, lam_k1, lam_q2, lam_k2, subln_w, w_o):
    h = x
    k = v = None
    for l in range(DEPTH):
        if l < N_A_LAYERS:
            h = h + mamba2_mixer(rms_norm(h, ssm_norm_w[l]), ssm_w_in[l], ssm_conv_w[l], ssm_conv_b[l],
                                 ssm_dt_bias[l], ssm_a_log[l], ssm_d[l], ssm_out_norm_w[l], ssm_w_out[l])
        else:
            j = l - N_A_LAYERS
            if j == 0:
                k, v = shared_kv(h, kv_norm_w, w_kv, k_norm_w)
            lambda_init = 0.8 - 0.6 * math.exp(-0.3 * l)
            h = h + diff_attention(rms_norm(h, attn_norm_w[j]), k, v, w_q[j], q_norm_w[j],
                                   lam_q1[j], lam_k1[j], lam_q2[j], lam_k2[j], subln_w[j], w_o[j],
                                   lambda_init)
        h = h + sq_relu_mlp(rms_norm(h, mlp_norm_w[l]), w_up[l], w_down[l])
    return h
```

```python
import functools
import math

import jax
import jax.numpy as jnp
from jax import lax
from jax.experimental import pallas as pl
from jax.experimental.pallas import tpu as pltpu

F32 = jnp.float32
BF16 = jnp.bfloat16
EPS = 1e-5

SSM_HEAD_DIM = 64
SSM_GROUPS = 8
SSM_STATE = 128
SSM_CONV = 4
DIFF_HEAD_DIM = 128
MASK_CHUNK = 64

LANES = 128
SUBLANES = 8
VMEM_LIMIT = 56 * 1024 * 1024

NEG_BIG = -1e30


def _params(sem):
    return pltpu.CompilerParams(dimension_semantics=sem, vmem_limit_bytes=VMEM_LIMIT)


def _rms_rows(x, w_row):
    ms = jnp.mean(x * x, axis=-1, keepdims=True)
    return x * lax.rsqrt(ms + EPS) * w_row


def _silu(x):
    return x * (1.0 / (1.0 + jnp.exp(-x)))


def _norm_matmul_kernel(x_ref, nw_ref, w_ref, sw_ref, o_ref, xn_ref, *, n_seg_tiles):
    j = pl.program_id(1)

    @pl.when(j == 0)
    def _():
        xn_ref[...] = _rms_rows(x_ref[...], nw_ref[...]).astype(BF16)

    acc = jnp.dot(xn_ref[...], w_ref[...], preferred_element_type=F32)

    if n_seg_tiles > 0:
        @pl.when(j < n_seg_tiles)
        def _():
            sw = sw_ref[...]
            for s in range(acc.shape[1] // LANES):
                blk = acc[:, s * LANES:(s + 1) * LANES]
                o_ref[:, s * LANES:(s + 1) * LANES] = _rms_rows(blk, sw).astype(o_ref.dtype)

        @pl.when(j >= n_seg_tiles)
        def _():
            o_ref[...] = acc.astype(o_ref.dtype)
    else:
        o_ref[...] = acc.astype(o_ref.dtype)


def _norm_matmul(x, nw, w, seg_w, *, tm, tn, n_seg_tiles):
    m, k = x.shape
    n = w.shape[1]
    return pl.pallas_call(
        functools.partial(_norm_matmul_kernel, n_seg_tiles=n_seg_tiles),
        grid=(m // tm, n // tn),
        in_specs=[
            pl.BlockSpec((tm, k), lambda i, j: (i, 0)),
            pl.BlockSpec((1, k), lambda i, j: (0, 0)),
            pl.BlockSpec((k, tn), lambda i, j: (0, j)),
            pl.BlockSpec((1, LANES), lambda i, j: (0, 0)),
        ],
        out_specs=pl.BlockSpec((tm, tn), lambda i, j: (i, j)),
        out_shape=jax.ShapeDtypeStruct((m, n), BF16),
        scratch_shapes=[pltpu.VMEM((tm, k), BF16)],
        compiler_params=_params(("parallel", "arbitrary")),
        name="norm_matmul",
    )(x, nw, w, seg_w)


def _in_proj_kernel(x_ref, nw_ref, w_ref, wdt_ref, dtb_ref, o_ref, dt_ref, xn_ref):
    j = pl.program_id(1)

    @pl.when(j == 0)
    def _():
        xn = _rms_rows(x_ref[...], nw_ref[...]).astype(BF16)
        xn_ref[...] = xn
        pre = jnp.dot(xn, wdt_ref[...], preferred_element_type=F32) + dtb_ref[...]
        dt_ref[...] = jnp.maximum(pre, 0.0) + jnp.log1p(jnp.exp(-jnp.abs(pre)))

    o_ref[...] = jnp.dot(xn_ref[...], w_ref[...], preferred_element_type=F32).astype(o_ref.dtype)


def _in_proj(x, nw, w, wdt, dtb, *, tm, tn):
    m, k = x.shape
    n = w.shape[1]
    return pl.pallas_call(
        _in_proj_kernel,
        grid=(m // tm, n // tn),
        in_specs=[
            pl.BlockSpec((tm, k), lambda i, j: (i, 0)),
            pl.BlockSpec((1, k), lambda i, j: (0, 0)),
            pl.BlockSpec((k, tn), lambda i, j: (0, j)),
            pl.BlockSpec((k, LANES), lambda i, j: (0, 0)),
            pl.BlockSpec((1, LANES), lambda i, j: (0, 0)),
        ],
        out_specs=[
            pl.BlockSpec((tm, tn), lambda i, j: (i, j)),
            pl.BlockSpec((tm, LANES), lambda i, j: (i, 0)),
        ],
        out_shape=[jax.ShapeDtypeStruct((m, n), BF16), jax.ShapeDtypeStruct((m, LANES), F32)],
        scratch_shapes=[pltpu.VMEM((tm, k), BF16)],
        compiler_params=_params(("parallel", "arbitrary")),
        name="in_proj",
    )(x, nw, w, wdt, dtb)


def _matmul_res_kernel(a_ref, w_ref, r_ref, o_ref):
    o_ref[...] = r_ref[...] + jnp.dot(a_ref[...], w_ref[...], preferred_element_type=F32)


def _matmul_res(a, w, res, *, tm, tn):
    m, k = a.shape
    n = w.shape[1]
    return pl.pallas_call(
        _matmul_res_kernel,
        grid=(m // tm, n // tn),
        in_specs=[
            pl.BlockSpec((tm, k), lambda i, j: (i, 0)),
            pl.BlockSpec((k, tn), lambda i, j: (0, j)),
            pl.BlockSpec((tm, tn), lambda i, j: (i, j)),
        ],
        out_specs=pl.BlockSpec((tm, tn), lambda i, j: (i, j)),
        out_shape=jax.ShapeDtypeStruct((m, n), F32),
        compiler_params=_params(("parallel", "arbitrary")),
        name="matmul_res",
    )(a, w, res)


def _mlp_kernel(x_ref, nw_ref, wu_ref, wd_ref, o_ref, xn_ref):
    j = pl.program_id(1)

    @pl.when(j == 0)
    def _():
        x = x_ref[...]
        xn_ref[...] = _rms_rows(x, nw_ref[...]).astype(BF16)
        o_ref[...] = x

    h = jnp.dot(xn_ref[...], wu_ref[...], preferred_element_type=F32)
    h = jnp.square(jnp.maximum(h, 0.0)).astype(BF16)
    o_ref[...] += jnp.dot(h, wd_ref[...], preferred_element_type=F32)


def _mlp(x, nw, wu, wd, *, tm, tf):
    m, d = x.shape
    ff = wu.shape[1]
    return pl.pallas_call(
        _mlp_kernel,
        grid=(m // tm, ff // tf),
        in_specs=[
            pl.BlockSpec((tm, d), lambda i, j: (i, 0)),
            pl.BlockSpec((1, d), lambda i, j: (0, 0)),
            pl.BlockSpec((d, tf), lambda i, j: (0, j)),
            pl.BlockSpec((tf, d), lambda i, j: (j, 0)),
        ],
        out_specs=pl.BlockSpec((tm, d), lambda i, j: (i, 0)),
        out_shape=jax.ShapeDtypeStruct((m, d), F32),
        scratch_shapes=[pltpu.VMEM((tm, d), BF16)],
        compiler_params=_params(("parallel", "arbitrary")),
        name="mlp",
    )(x, nw, wu, wd)


def _cumsum_rows(a, row):
    n = a.shape[0]
    sh = 1
    while sh < n:
        a = a + jnp.where(row >= sh, pltpu.roll(a, sh, axis=0), 0.0)
        sh *= 2
    return a


def _ssd_kernel(z_ref, xs_ref, bc_ref, dt_ref, cw_ref, cb_ref, alog_ref, dexp_ref, onw_ref, e_ref,
                o_ref, st_ref, extx_ref, extbc_ref, xc_ref, bcc_ref, ex_ref, *, L, d_inner, gn):
    G, N, P = SSM_GROUPS, SSM_STATE, SSM_HEAD_DIM
    R = d_inner // (G * P)
    GW = R * P
    T = SUBLANES
    c = pl.program_id(1)

    @pl.when(c == 0)
    def _():
        st_ref[...] = jnp.zeros(st_ref.shape, F32)
        extx_ref[0:T, :] = jnp.zeros((T, d_inner), F32)
        extbc_ref[0:T, :] = jnp.zeros((T, 2 * gn), F32)

    extx_ref[T:T + L, :] = xs_ref[...].astype(F32)
    extbc_ref[T:T + L, :] = bc_ref[...].astype(F32)
    slab = 512
    for s in range(d_inner // slab):
        cols = slice(s * slab, (s + 1) * slab)
        acc = cb_ref[:, cols]
        for k in range(SSM_CONV):
            off = T - (SSM_CONV - 1) + k
            acc = acc + cw_ref[k:k + 1, cols] * extx_ref[off:off + L, cols]
        xc_ref[:, cols] = _silu(acc)
    for s in range(2 * gn // slab):
        cols = slice(s * slab, (s + 1) * slab)
        wcols = slice(d_inner + s * slab, d_inner + (s + 1) * slab)
        acc = cb_ref[:, wcols]
        for k in range(SSM_CONV):
            off = T - (SSM_CONV - 1) + k
            acc = acc + cw_ref[k:k + 1, wcols] * extbc_ref[off:off + L, cols]
        bcc_ref[:, cols] = _silu(acc)
    extx_ref[0:T, :] = extx_ref[L:L + T, :]
    extbc_ref[0:T, :] = extbc_ref[L:L + T, :]

    row = lax.broadcasted_iota(jnp.int32, (L, LANES), 0)
    dt = dt_ref[...]
    da = dt * (-jnp.exp(alog_ref[...]))
    a_cs = _cumsum_rows(da, row)
    a_last = a_cs[L - 1:L, :]
    w_end = jnp.exp(a_last - a_cs) * dt
    ea = jnp.exp(a_cs)
    cd = jnp.broadcast_to(jnp.exp(a_last), (SUBLANES, LANES))
    stacked = jnp.concatenate([w_end, ea, cd], axis=0)
    hi = stacked.astype(BF16)
    lo = (stacked - hi.astype(F32)).astype(BF16)
    e = e_ref[...]
    ex_ref[...] = (jnp.dot(hi, e, preferred_element_type=F32)
                   + jnp.dot(lo, e, preferred_element_type=F32))
    a_cs_t = a_cs.T
    dt_t = dt.T

    li = lax.broadcasted_iota(jnp.int32, (L, L), 0)
    si = lax.broadcasted_iota(jnp.int32, (L, L), 1)
    tril = si <= li
    lane = lax.broadcasted_iota(jnp.int32, (L, 2 * P), 1)
    first_head = lane < P

    for g in range(G):
        gcols = slice(g * GW, (g + 1) * GW)
        b_f = bcc_ref[:, g * N:(g + 1) * N]
        c_bf = bcc_ref[:, gn + g * N:gn + (g + 1) * N].astype(BF16)
        b_t = b_f.T.astype(BF16)
        cb = jnp.dot(c_bf, b_t, preferred_element_type=F32)
        xg = xc_ref[:, gcols]
        xg_bf = xg.astype(BF16)

        st_old = st_ref[g]
        y = jnp.dot(c_bf, st_old.astype(BF16), preferred_element_type=F32) * ex_ref[L:2 * L, gcols]
        y = y + xg * dexp_ref[:, gcols]

        pairs = []
        for pr in range(R // 2):
            x_pair = xg_bf[:, pr * 2 * P:(pr + 1) * 2 * P]
            acc = None
            for q in range(2):
                h = g * R + pr * 2 + q
                seg = a_cs[:, h:h + 1] - a_cs_t[h:h + 1, :]
                decay = jnp.exp(jnp.where(tril, seg, -jnp.inf))
                m_h = (cb * decay * dt_t[h:h + 1, :]).astype(BF16)
                keep = first_head if q == 0 else jnp.logical_not(first_head)
                x_h = jnp.where(keep, x_pair, jnp.zeros_like(x_pair))
                d = jnp.dot(m_h, x_h, preferred_element_type=F32)
                acc = d if acc is None else acc + d
            pairs.append(acc)
        y = y + jnp.concatenate(pairs, axis=1)

        xw = (xg * ex_ref[0:L, gcols]).astype(BF16)
        st_ref[g] = st_old * ex_ref[2 * L:2 * L + 1, gcols] + jnp.dot(b_t, xw, preferred_element_type=F32)

        zg = z_ref[:, gcols].astype(F32)
        gt = y * _silu(zg)
        o_ref[:, gcols] = _rms_rows(gt, onw_ref[:, gcols]).astype(o_ref.dtype)


def _ssd(zx, dt, cw, cb, alog, dexp, onw, e, *, batch, seq, L, d_inner, gn):
    m = zx.shape[0]
    nc = seq // L
    nz = d_inner // (2 * gn)
    assert d_inner % (2 * gn) == 0
    kern = functools.partial(_ssd_kernel, L=L, d_inner=d_inner, gn=gn)
    row_map = lambda b, c: (b * nc + c, 0)
    const = lambda b, c: (0, 0)
    return pl.pallas_call(
        kern,
        grid=(batch, nc),
        in_specs=[
            pl.BlockSpec((L, d_inner), row_map),
            pl.BlockSpec((L, d_inner), lambda b, c: (b * nc + c, 1)),
            pl.BlockSpec((L, 2 * gn), lambda b, c: (b * nc + c, 2 * nz)),
            pl.BlockSpec((L, LANES), row_map),
            pl.BlockSpec(cw.shape, const),
            pl.BlockSpec(cb.shape, const),
            pl.BlockSpec(alog.shape, const),
            pl.BlockSpec(dexp.shape, const),
            pl.BlockSpec(onw.shape, const),
            pl.BlockSpec(e.shape, const),
        ],
        out_specs=pl.BlockSpec((L, d_inner), row_map),
        out_shape=jax.ShapeDtypeStruct((m, d_inner), BF16),
        scratch_shapes=[
            pltpu.VMEM((SSM_GROUPS, SSM_STATE, d_inner // SSM_GROUPS), F32),
            pltpu.VMEM((L + SUBLANES, d_inner), F32),
            pltpu.VMEM((L + SUBLANES, 2 * gn), F32),
            pltpu.VMEM((L, d_inner), F32),
            pltpu.VMEM((L, 2 * gn), F32),
            pltpu.VMEM((2 * L + SUBLANES, d_inner), F32),
        ],
        compiler_params=_params(("arbitrary", "arbitrary")),
        name="ssd",
    )(zx, zx, zx, dt, cw, cb, alog, dexp, onw, e)


def _attn_kernel(q_ref, k_ref, v_ref, lq1_ref, lk1_ref, lq2_ref, lk2_ref, sw_ref, o_ref,
                 m_ref, l_ref, acc_ref, *, t, lambda_init):
    D = DIFF_HEAD_DIM
    qi = pl.program_id(2)
    m_ref[...] = jnp.full(m_ref.shape, NEG_BIG, F32)
    l_ref[...] = jnp.zeros(l_ref.shape, F32)
    acc_ref[...] = jnp.zeros(acc_ref.shape, F32)

    def tile(kj, masked):
        start = pl.multiple_of(kj * t, t)
        k = k_ref[pl.ds(start, t), :]
        v = v_ref[pl.ds(start, t), :]
        for c in range(2):
            q = q_ref[:, c * D:(c + 1) * D]
            s = lax.dot_general(q, k[:, c * D:(c + 1) * D], (((1,), (1,)), ((), ())),
                                preferred_element_type=F32)
            if masked:
                qc = lax.broadcasted_iota(jnp.int32, (t, t), 0) // MASK_CHUNK
                kc = lax.broadcasted_iota(jnp.int32, (t, t), 1) // MASK_CHUNK
                s = jnp.where(kc <= qc, s, NEG_BIG)
            m_old = m_ref[c]
            m_new = jnp.maximum(m_old, jnp.max(s, axis=-1, keepdims=True))
            alpha = jnp.exp2(m_old - m_new)
            p = jnp.exp2(s - m_new)
            l_ref[c] = alpha * l_ref[c] + jnp.sum(p, axis=-1, keepdims=True)
            acc_ref[c] = alpha * acc_ref[c] + jnp.dot(p.astype(BF16), v, preferred_element_type=F32)
            m_ref[c] = m_new

    def body(kj, carry):
        tile(kj, False)
        return carry

    lax.fori_loop(0, qi, body, 0)
    tile(qi, True)

    lam = (jnp.exp(jnp.sum(lq1_ref[...] * lk1_ref[...], axis=-1, keepdims=True))
           - jnp.exp(jnp.sum(lq2_ref[...] * lk2_ref[...], axis=-1, keepdims=True))
           + lambda_init)
    o = acc_ref[0] / l_ref[0] - lam * (acc_ref[1] / l_ref[1])
    o_ref[...] = (_rms_rows(o, sw_ref[...]) * (1.0 - lambda_init)).astype(o_ref.dtype)


def _attention(q, kv, lq1, lk1, lq2, lk2, sw, *, batch, seq, t, lambda_init):
    m, d = q.shape
    hw = 2 * DIFF_HEAD_DIM
    heads = d // hw
    nq = seq // t
    vec = pl.BlockSpec((1, DIFF_HEAD_DIM), lambda b, h, i: (0, 0))
    k_spec = pl.BlockSpec((seq, hw), lambda b, h, i: (b, h))
    v_spec = pl.BlockSpec((seq, hw), lambda b, h, i: (b, heads + h))
    return pl.pallas_call(
        functools.partial(_attn_kernel, t=t, lambda_init=lambda_init),
        grid=(batch, heads, nq),
        in_specs=[
            pl.BlockSpec((t, hw), lambda b, h, i: (b * nq + i, h)),
            k_spec, v_spec, vec, vec, vec, vec,
            pl.BlockSpec((1, hw), lambda b, h, i: (0, 0)),
        ],
        out_specs=pl.BlockSpec((t, hw), lambda b, h, i: (b * nq + i, h)),
        out_shape=jax.ShapeDtypeStruct((m, d), BF16),
        scratch_shapes=[
            pltpu.VMEM((2, t, 1), F32),
            pltpu.VMEM((2, t, 1), F32),
            pltpu.VMEM((2, t, hw), F32),
        ],
        compiler_params=_params(("parallel", "parallel", "arbitrary")),
        name="diff_attn",
    )(q, kv, kv, lq1, lk1, lq2, lk2, sw)


TM = 1024
TN_PROJ = 1024
TF_MLP = 512
SSD_CHUNK = 128
ATTN_TILE = 512


def kernel(x, mlp_norm_w, w_up, w_down, ssm_norm_w, ssm_w_in, ssm_conv_w, ssm_conv_b, ssm_dt_bias, ssm_a_log,
           ssm_d, ssm_out_norm_w, ssm_w_out, kv_norm_w, w_kv, k_norm_w, attn_norm_w, w_q, q_norm_w, lam_q1,
           lam_k1, lam_q2, lam_k2, subln_w, w_o):
    batch, seq, d_model = x.shape
    m = batch * seq
    depth = mlp_norm_w.shape[0]
    n_a = ssm_norm_w.shape[0]
    n_heads = ssm_dt_bias.shape[1]
    d_inner = n_heads * SSM_HEAD_DIM
    gn = SSM_GROUPS * SSM_STATE
    tm = min(TM, m)

    h = x.reshape(m, d_model)
    row = lambda a: a.reshape(1, -1).astype(F32)

    head_of_col = jnp.arange(d_inner, dtype=jnp.int32) // SSM_HEAD_DIM
    expand = (jnp.arange(LANES, dtype=jnp.int32)[:, None] == head_of_col[None, :]).astype(BF16)
    pad_heads = lambda a: jnp.pad(a.astype(F32), (0, LANES - n_heads)).reshape(1, LANES)

    kv = None
    for l in range(depth):
        if l < n_a:
            w_in = ssm_w_in[l]
            w_zx = w_in[:, :2 * d_inner + 2 * gn].astype(BF16)
            w_dt = jnp.pad(w_in[:, 2 * d_inner + 2 * gn:], ((0, 0), (0, LANES - n_heads))).astype(BF16)
            zx, dt = _in_proj(h, row(ssm_norm_w[l]), w_zx, w_dt, pad_heads(ssm_dt_bias[l]), tm=tm, tn=TN_PROJ)
            y = _ssd(zx, dt, ssm_conv_w[l].astype(F32), row(ssm_conv_b[l]), pad_heads(ssm_a_log[l]),
                     row(jnp.repeat(ssm_d[l], SSM_HEAD_DIM)), row(ssm_out_norm_w[l]), expand,
                     batch=batch, seq=seq, L=min(SSD_CHUNK, seq), d_inner=d_inner, gn=gn)
            h = _matmul_res(y, ssm_w_out[l].astype(BF16), h, tm=tm, tn=TN_PROJ)
        else:
            j = l - n_a
            if j == 0:
                kv = _norm_matmul(h, row(kv_norm_w), w_kv.astype(BF16), row(k_norm_w), tm=tm, tn=TN_PROJ,
                                  n_seg_tiles=d_model // TN_PROJ)
            lambda_init = 0.8 - 0.6 * math.exp(-0.3 * l)
            q_scale = DIFF_HEAD_DIM ** -0.5 * math.log2(math.e)
            q = _norm_matmul(h, row(attn_norm_w[j]), w_q[j].astype(BF16), row(q_norm_w[j]) * q_scale,
                             tm=tm, tn=TN_PROJ, n_seg_tiles=d_model // TN_PROJ)
            o = _attention(q, kv, row(lam_q1[j]), row(lam_k1[j]), row(lam_q2[j]), row(lam_k2[j]),
                           row(subln_w[j]), batch=batch, seq=seq, t=min(ATTN_TILE, seq), lambda_init=lambda_init)
            h = _matmul_res(o, w_o[j].astype(BF16), h, tm=tm, tn=TN_PROJ)
        h = _mlp(h, row(mlp_norm_w[l]), w_up[l].astype(BF16), w_down[l].astype(BF16), tm=tm, tf=TF_MLP)
    return h.reshape(batch, seq, d_model)
```

```python
import functools
import math

import jax
import jax.numpy as jnp
from jax import lax
from jax.experimental import pallas as pl
from jax.experimental.pallas import tpu as pltpu

F32 = jnp.float32
BF16 = jnp.bfloat16
EPS = 1e-5

SSM_HEAD_DIM = 64
SSM_GROUPS = 8
SSM_STATE = 128
SSM_CONV = 4
DIFF_HEAD_DIM = 128
MASK_CHUNK = 64

LANES = 128
SUBLANES = 8
VMEM_LIMIT = 56 * 1024 * 1024

NEG_BIG = -1e30


def _params(sem):
    return pltpu.CompilerParams(dimension_semantics=sem, vmem_limit_bytes=VMEM_LIMIT)


def _rms_rows(x, w_row):
    ms = jnp.mean(x * x, axis=-1, keepdims=True)
    return x * lax.rsqrt(ms + EPS) * w_row


def _silu(x):
    return x * (1.0 / (1.0 + jnp.exp(-x)))


def _norm_matmul_kernel(x_ref, nw_ref, w_ref, sw_ref, o_ref, xn_ref, *, n_seg_tiles):
    j = pl.program_id(1)

    @pl.when(j == 0)
    def _():
        xn_ref[...] = _rms_rows(x_ref[...], nw_ref[...]).astype(BF16)

    acc = jnp.dot(xn_ref[...], w_ref[...], preferred_element_type=F32)

    if n_seg_tiles > 0:
        @pl.when(j < n_seg_tiles)
        def _():
            sw = sw_ref[...]
            for s in range(acc.shape[1] // LANES):
                blk = acc[:, s * LANES:(s + 1) * LANES]
                o_ref[:, s * LANES:(s + 1) * LANES] = _rms_rows(blk, sw).astype(o_ref.dtype)

        @pl.when(j >= n_seg_tiles)
        def _():
            o_ref[...] = acc.astype(o_ref.dtype)
    else:
        o_ref[...] = acc.astype(o_ref.dtype)


def _norm_matmul(x, nw, w, seg_w, *, tm, tn, n_seg_tiles):
    m, k = x.shape
    n = w.shape[1]
    return pl.pallas_call(
        functools.partial(_norm_matmul_kernel, n_seg_tiles=n_seg_tiles),
        grid=(m // tm, n // tn),
        in_specs=[
            pl.BlockSpec((tm, k), lambda i, j: (i, 0)),
            pl.BlockSpec((1, k), lambda i, j: (0, 0)),
            pl.BlockSpec((k, tn), lambda i, j: (0, j)),
            pl.BlockSpec((1, LANES), lambda i, j: (0, 0)),
        ],
        out_specs=pl.BlockSpec((tm, tn), lambda i, j: (i, j)),
        out_shape=jax.ShapeDtypeStruct((m, n), BF16),
        scratch_shapes=[pltpu.VMEM((tm, k), BF16)],
        compiler_params=_params(("parallel", "arbitrary")),
        name="norm_matmul",
    )(x, nw, w, seg_w)


def _in_proj_kernel(x_ref, nw_ref, w_ref, wdt_ref, dtb_ref, o_ref, dt_ref, xn_ref):
    j = pl.program_id(1)

    @pl.when(j == 0)
    def _():
        xn = _rms_rows(x_ref[...], nw_ref[...]).astype(BF16)
        xn_ref[...] = xn
        pre = jnp.dot(xn, wdt_ref[...], preferred_element_type=F32) + dtb_ref[...]
        dt_ref[...] = jnp.maximum(pre, 0.0) + jnp.log1p(jnp.exp(-jnp.abs(pre)))

    o_ref[...] = jnp.dot(xn_ref[...], w_ref[...], preferred_element_type=F32).astype(o_ref.dtype)


def _in_proj(x, nw, w, wdt, dtb, *, tm, tn):
    m, k = x.shape
    n = w.shape[1]
    return pl.pallas_call(
        _in_proj_kernel,
        grid=(m // tm, n // tn),
        in_specs=[
            pl.BlockSpec((tm, k), lambda i, j: (i, 0)),
            pl.BlockSpec((1, k), lambda i, j: (0, 0)),
            pl.BlockSpec((k, tn), lambda i, j: (0, j)),
            pl.BlockSpec((k, LANES), lambda i, j: (0, 0)),
            pl.BlockSpec((1, LANES), lambda i, j: (0, 0)),
        ],
        out_specs=[
            pl.BlockSpec((tm, tn), lambda i, j: (i, j)),
            pl.BlockSpec((tm, LANES), lambda i, j: (i, 0)),
        ],
        out_shape=[jax.ShapeDtypeStruct((m, n), BF16), jax.ShapeDtypeStruct((m, LANES), F32)],
        scratch_shapes=[pltpu.VMEM((tm, k), BF16)],
        compiler_params=_params(("parallel", "arbitrary")),
        name="in_proj",
    )(x, nw, w, wdt, dtb)


def _matmul_res_kernel(a_ref, w_ref, r_ref, o_ref):
    o_ref[...] = r_ref[...] + jnp.dot(a_ref[...], w_ref[...], preferred_element_type=F32)


def _matmul_res(a, w, res, *, tm, tn):
    m, k = a.shape
    n = w.shape[1]
    return pl.pallas_call(
        _matmul_res_kernel,
        grid=(m // tm, n // tn),
        in_specs=[
            pl.BlockSpec((tm, k), lambda i, j: (i, 0)),
            pl.BlockSpec((k, tn), lambda i, j: (0, j)),
            pl.BlockSpec((tm, tn), lambda i, j: (i, j)),
        ],
        out_specs=pl.BlockSpec((tm, tn), lambda i, j: (i, j)),
        out_shape=jax.ShapeDtypeStruct((m, n), F32),
        compiler_params=_params(("parallel", "arbitrary")),
        name="matmul_res",
    )(a, w, res)


def _mlp_kernel(x_ref, nw_ref, wu_ref, wd_ref, o_ref, xn_ref):
    j = pl.program_id(1)

    @pl.when(j == 0)
    def _():
        x = x_ref[...]
        xn_ref[...] = _rms_rows(x, nw_ref[...]).astype(BF16)
        o_ref[...] = x

    h = jnp.dot(xn_ref[...], wu_ref[...], preferred_element_type=F32)
    h = jnp.square(jnp.maximum(h, 0.0)).astype(BF16)
    o_ref[...] += jnp.dot(h, wd_ref[...], preferred_element_type=F32)


def _mlp(x, nw, wu, wd, *, tm, tf):
    m, d = x.shape
    ff = wu.shape[1]
    return pl.pallas_call(
        _mlp_kernel,
        grid=(m // tm, ff // tf),
        in_specs=[
            pl.BlockSpec((tm, d), lambda i, j: (i, 0)),
            pl.BlockSpec((1, d), lambda i, j: (0, 0)),
            pl.BlockSpec((d, tf), lambda i, j: (0, j)),
            pl.BlockSpec((tf, d), lambda i, j: (j, 0)),
        ],
        out_specs=pl.BlockSpec((tm, d), lambda i, j: (i, 0)),
        out_shape=jax.ShapeDtypeStruct((m, d), F32),
        scratch_shapes=[pltpu.VMEM((tm, d), BF16)],
        compiler_params=_params(("parallel", "arbitrary")),
        name="mlp",
    )(x, nw, wu, wd)


def _cumsum_rows(a, row):
    n = a.shape[0]
    sh = 1
    while sh < n:
        a = a + jnp.where(row >= sh, pltpu.roll(a, sh, axis=0), 0.0)
        sh *= 2
    return a


def _ssd_kernel(z_ref, xs_ref, bc_ref, dt_ref, cw_ref, cb_ref, alog_ref, dexp_ref, onw_ref, e_ref,
                o_ref, st_ref, extx_ref, extbc_ref, xc_ref, bcc_ref, ex_ref, *, L, d_inner, gn):
    G, N, P = SSM_GROUPS, SSM_STATE, SSM_HEAD_DIM
    R = d_inner // (G * P)
    GW = R * P
    T = SUBLANES
    c = pl.program_id(1)

    @pl.when(c == 0)
    def _():
        st_ref[...] = jnp.zeros(st_ref.shape, F32)
        extx_ref[0:T, :] = jnp.zeros((T, d_inner), F32)
        extbc_ref[0:T, :] = jnp.zeros((T, 2 * gn), F32)

    extx_ref[T:T + L, :] = xs_ref[...].astype(F32)
    extbc_ref[T:T + L, :] = bc_ref[...].astype(F32)
    slab = 512
    for s in range(d_inner // slab):
        cols = slice(s * slab, (s + 1) * slab)
        acc = cb_ref[:, cols]
        for k in range(SSM_CONV):
            off = T - (SSM_CONV - 1) + k
            acc = acc + cw_ref[k:k + 1, cols] * extx_ref[off:off + L, cols]
        xc_ref[:, cols] = _silu(acc)
    for s in range(2 * gn // slab):
        cols = slice(s * slab, (s + 1) * slab)
        wcols = slice(d_inner + s * slab, d_inner + (s + 1) * slab)
        acc = cb_ref[:, wcols]
        for k in range(SSM_CONV):
            off = T - (SSM_CONV - 1) + k
            acc = acc + cw_ref[k:k + 1, wcols] * extbc_ref[off:off + L, cols]
        bcc_ref[:, cols] = _silu(acc)
    extx_ref[0:T, :] = extx_ref[L:L + T, :]
    extbc_ref[0:T, :] = extbc_ref[L:L + T, :]

    row = lax.broadcasted_iota(jnp.int32, (L, LANES), 0)
    dt = dt_ref[...]
    da = dt * (-jnp.exp(alog_ref[...]))
    a_cs = _cumsum_rows(da, row)
    a_last = a_cs[L - 1:L, :]
    w_end = jnp.exp(a_last - a_cs) * dt
    ea = jnp.exp(a_cs)
    cd = jnp.broadcast_to(jnp.exp(a_last), (SUBLANES, LANES))
    stacked = jnp.concatenate([w_end, ea, cd], axis=0)
    hi = stacked.astype(BF16)
    lo = (stacked - hi.astype(F32)).astype(BF16)
    e = e_ref[...]
    ex_ref[...] = (jnp.dot(hi, e, preferred_element_type=F32)
                   + jnp.dot(lo, e, preferred_element_type=F32))
    a_cs_t = a_cs.T
    dt_t = dt.T

    li = lax.broadcasted_iota(jnp.int32, (L, L), 0)
    si = lax.broadcasted_iota(jnp.int32, (L, L), 1)
    tril = si <= li
    lane = lax.broadcasted_iota(jnp.int32, (L, 2 * P), 1)
    first_head = lane < P

    for g in range(G):
        gcols = slice(g * GW, (g + 1) * GW)
        b_f = bcc_ref[:, g * N:(g + 1) * N]
        c_bf = bcc_ref[:, gn + g * N:gn + (g + 1) * N].astype(BF16)
        b_t = b_f.T.astype(BF16)
        cb = jnp.dot(c_bf, b_t, preferred_element_type=F32)
        xg = xc_ref[:, gcols]
        xg_bf = xg.astype(BF16)

        st_old = st_ref[g]
        y = jnp.dot(c_bf, st_old.astype(BF16), preferred_element_type=F32) * ex_ref[L:2 * L, gcols]
        y = y + xg * dexp_ref[:, gcols]

        pairs = []
        for pr in range(R // 2):
            x_pair = xg_bf[:, pr * 2 * P:(pr + 1) * 2 * P]
            acc = None
            for q in range(2):
                h = g * R + pr * 2 + q
                seg = a_cs[:, h:h + 1] - a_cs_t[h:h + 1, :]
                decay = jnp.exp(jnp.where(tril, seg, -jnp.inf))
                m_h = (cb * decay * dt_t[h:h + 1, :]).astype(BF16)
                keep = first_head if q == 0 else jnp.logical_not(first_head)
                x_h = jnp.where(keep, x_pair, jnp.zeros_like(x_pair))
                d = jnp.dot(m_h, x_h, preferred_element_type=F32)
                acc = d if acc is None else acc + d
            pairs.append(acc)
        y = y + jnp.concatenate(pairs, axis=1)

        xw = (xg * ex_ref[0:L, gcols]).astype(BF16)
        st_ref[g] = st_old * ex_ref[2 * L:2 * L + 1, gcols] + jnp.dot(b_t, xw, preferred_element_type=F32)

        zg = z_ref[:, gcols].astype(F32)
        gt = y * _silu(zg)
        o_ref[:, gcols] = _rms_rows(gt, onw_ref[:, gcols]).astype(o_ref.dtype)


def _ssd(zx, dt, cw, cb, alog, dexp, onw, e, *, batch, seq, L, d_inner, gn):
    m = zx.shape[0]
    nc = seq // L
    nz = d_inner // (2 * gn)
    assert d_inner % (2 * gn) == 0
    kern = functools.partial(_ssd_kernel, L=L, d_inner=d_inner, gn=gn)
    row_map = lambda b, c: (b * nc + c, 0)
    const = lambda b, c: (0, 0)
    return pl.pallas_call(
        kern,
        grid=(batch, nc),
        in_specs=[
            pl.BlockSpec((L, d_inner), row_map),
            pl.BlockSpec((L, d_inner), lambda b, c: (b * nc + c, 1)),
            pl.BlockSpec((L, 2 * gn), lambda b, c: (b * nc + c, 2 * nz)),
            pl.BlockSpec((L, LANES), row_map),
            pl.BlockSpec(cw.shape, const),
            pl.BlockSpec(cb.shape, const),
            pl.BlockSpec(alog.shape, const),
            pl.BlockSpec(dexp.shape, const),
            pl.BlockSpec(onw.shape, const),
            pl.BlockSpec(e.shape, const),
        ],
        out_specs=pl.BlockSpec((L, d_inner), row_map),
        out_shape=jax.ShapeDtypeStruct((m, d_inner), BF16),
        scratch_shapes=[
            pltpu.VMEM((SSM_GROUPS, SSM_STATE, d_inner // SSM_GROUPS), F32),
            pltpu.VMEM((L + SUBLANES, d_inner), F32),
            pltpu.VMEM((L + SUBLANES, 2 * gn), F32),
            pltpu.VMEM((L, d_inner), F32),
            pltpu.VMEM((L, 2 * gn), F32),
            pltpu.VMEM((2 * L + SUBLANES, d_inner), F32),
        ],
        compiler_params=_params(("arbitrary", "arbitrary")),
        name="ssd",
    )(zx, zx, zx, dt, cw, cb, alog, dexp, onw, e)


def _attn_kernel(q_ref, k_ref, v_ref, lq1_ref, lk1_ref, lq2_ref, lk2_ref, sw_ref, o_ref,
                 sa_ref, sb_ref, p_ref, a_ref, m_ref, l_ref, acc_ref, *, t, rc, lambda_init):
    D = DIFF_HEAD_DIM
    nl = t // LANES
    qi = pl.program_id(2)
    m_ref[...] = jnp.full(m_ref.shape, NEG_BIG, F32)
    l_ref[...] = jnp.zeros(l_ref.shape, F32)
    acc_ref[...] = jnp.zeros(acc_ref.shape, F32)

    def scores(kj, s_ref):
        start = pl.multiple_of(kj * t, t)
        for c in range(2):
            k = k_ref[pl.ds(start, t), c * D:(c + 1) * D]
            s_ref[c] = lax.dot_general(q_ref[:, c * D:(c + 1) * D], k, (((1,), (1,)), ((), ())),
                                       preferred_element_type=F32)

    def softmax_pv(kj, s_ref, masked):
        start = pl.multiple_of(kj * t, t)
        v = v_ref[pl.ds(start, t), :]
        for c in range(2):
            for r in range(t // rc):
                rows = slice(r * rc, (r + 1) * rc)
                s = s_ref[c, rows, :]
                if masked:
                    qc = (lax.broadcasted_iota(jnp.int32, (rc, t), 0) + r * rc) // MASK_CHUNK
                    kc = lax.broadcasted_iota(jnp.int32, (rc, t), 1) // MASK_CHUNK
                    s = jnp.where(kc <= qc, s, NEG_BIG)
                m_old = m_ref[c, rows, :]
                m_new = jnp.maximum(m_old, jnp.broadcast_to(jnp.max(s, axis=1, keepdims=True), (rc, LANES)))
                alpha = jnp.exp2(m_old - m_new)
                ps = [jnp.exp2(s[:, i * LANES:(i + 1) * LANES] - m_new) for i in range(nl)]
                psum = ps[0]
                for i in range(1, nl):
                    psum = psum + ps[i]
                l_ref[c, rows, :] = alpha * l_ref[c, rows, :] + psum
                m_ref[c, rows, :] = m_new
                a_ref[c, rows, :] = alpha
                p_ref[c, rows, :] = jnp.concatenate(ps, axis=1).astype(BF16)
            pv = jnp.dot(p_ref[c], v, preferred_element_type=F32)
            alpha = a_ref[c]
            acc = acc_ref[c]
            acc_ref[c] = jnp.concatenate([acc[:, i * LANES:(i + 1) * LANES] * alpha
                                          for i in range(2 * D // LANES)], axis=1) + pv

    scores(0, sa_ref)

    def body(i, carry):
        kj = 2 * i
        scores(kj + 1, sb_ref)
        softmax_pv(kj, sa_ref, False)
        scores(kj + 2, sa_ref)
        softmax_pv(kj + 1, sb_ref, False)
        return carry

    lax.fori_loop(0, qi // 2, body, 0)

    @pl.when(qi % 2 == 0)
    def _():
        softmax_pv(qi, sa_ref, True)

    @pl.when(qi % 2 == 1)
    def _():
        scores(qi, sb_ref)
        softmax_pv(qi - 1, sa_ref, False)
        softmax_pv(qi, sb_ref, True)

    lam = (jnp.exp(jnp.sum(lq1_ref[...] * lk1_ref[...], axis=-1, keepdims=True))
           - jnp.exp(jnp.sum(lq2_ref[...] * lk2_ref[...], axis=-1, keepdims=True))
           + lambda_init)
    l0 = jnp.sum(l_ref[0], axis=-1, keepdims=True)
    l1 = jnp.sum(l_ref[1], axis=-1, keepdims=True)
    o = acc_ref[0] / l0 - lam * (acc_ref[1] / l1)
    o_ref[...] = (_rms_rows(o, sw_ref[...]) * (1.0 - lambda_init)).astype(o_ref.dtype)


def _attention(q, kv, lq1, lk1, lq2, lk2, sw, *, batch, seq, t, lambda_init):
    m, d = q.shape
    hw = 2 * DIFF_HEAD_DIM
    heads = d // hw
    nq = seq // t
    vec = pl.BlockSpec((1, DIFF_HEAD_DIM), lambda b, h, i: (0, 0))
    k_spec = pl.BlockSpec((seq, hw), lambda b, h, i: (b, h))
    v_spec = pl.BlockSpec((seq, hw), lambda b, h, i: (b, heads + h))
    return pl.pallas_call(
        functools.partial(_attn_kernel, t=t, rc=min(ATTN_ROW_CHUNK, t), lambda_init=lambda_init),
        grid=(batch, heads, nq),
        in_specs=[
            pl.BlockSpec((t, hw), lambda b, h, i: (b * nq + i, h)),
            k_spec, v_spec, vec, vec, vec, vec,
            pl.BlockSpec((1, hw), lambda b, h, i: (0, 0)),
        ],
        out_specs=pl.BlockSpec((t, hw), lambda b, h, i: (b * nq + i, h)),
        out_shape=jax.ShapeDtypeStruct((m, d), BF16),
        scratch_shapes=[
            pltpu.VMEM((2, t, t), F32),
            pltpu.VMEM((2, t, t), F32),
            pltpu.VMEM((2, t, t), BF16),
            pltpu.VMEM((2, t, LANES), F32),
            pltpu.VMEM((2, t, LANES), F32),
            pltpu.VMEM((2, t, LANES), F32),
            pltpu.VMEM((2, t, hw), F32),
        ],
        compiler_params=_params(("parallel", "parallel", "arbitrary")),
        name="diff_attn",
    )(q, kv, kv, lq1, lk1, lq2, lk2, sw)


TM = 1024
TN_PROJ = 1024
TF_MLP = 512
SSD_CHUNK = 128
ATTN_TILE = 512
ATTN_ROW_CHUNK = 32


def kernel(x, mlp_norm_w, w_up, w_down, ssm_norm_w, ssm_w_in, ssm_conv_w, ssm_conv_b, ssm_dt_bias, ssm_a_log,
           ssm_d, ssm_out_norm_w, ssm_w_out, kv_norm_w, w_kv, k_norm_w, attn_norm_w, w_q, q_norm_w, lam_q1,
           lam_k1, lam_q2, lam_k2, subln_w, w_o):
    batch, seq, d_model = x.shape
    m = batch * seq
    depth = mlp_norm_w.shape[0]
    n_a = ssm_norm_w.shape[0]
    n_heads = ssm_dt_bias.shape[1]
    d_inner = n_heads * SSM_HEAD_DIM
    gn = SSM_GROUPS * SSM_STATE
    tm = min(TM, m)

    h = x.reshape(m, d_model)
    row = lambda a: a.reshape(1, -1).astype(F32)

    head_of_col = jnp.arange(d_inner, dtype=jnp.int32) // SSM_HEAD_DIM
    expand = (jnp.arange(LANES, dtype=jnp.int32)[:, None] == head_of_col[None, :]).astype(BF16)
    pad_heads = lambda a: jnp.pad(a.astype(F32), (0, LANES - n_heads)).reshape(1, LANES)

    kv = None
    for l in range(depth):
        if l < n_a:
            w_in = ssm_w_in[l]
            w_zx = w_in[:, :2 * d_inner + 2 * gn].astype(BF16)
            w_dt = jnp.pad(w_in[:, 2 * d_inner + 2 * gn:], ((0, 0), (0, LANES - n_heads))).astype(BF16)
            zx, dt = _in_proj(h, row(ssm_norm_w[l]), w_zx, w_dt, pad_heads(ssm_dt_bias[l]), tm=tm, tn=TN_PROJ)
            y = _ssd(zx, dt, ssm_conv_w[l].astype(F32), row(ssm_conv_b[l]), pad_heads(ssm_a_log[l]),
                     row(jnp.repeat(ssm_d[l], SSM_HEAD_DIM)), row(ssm_out_norm_w[l]), expand,
                     batch=batch, seq=seq, L=min(SSD_CHUNK, seq), d_inner=d_inner, gn=gn)
            h = _matmul_res(y, ssm_w_out[l].astype(BF16), h, tm=tm, tn=TN_PROJ)
        else:
            j = l - n_a
            if j == 0:
                kv = _norm_matmul(h, row(kv_norm_w), w_kv.astype(BF16), row(k_norm_w), tm=tm, tn=TN_PROJ,
                                  n_seg_tiles=d_model // TN_PROJ)
            lambda_init = 0.8 - 0.6 * math.exp(-0.3 * l)
            q_scale = DIFF_HEAD_DIM ** -0.5 * math.log2(math.e)
            q = _norm_matmul(h, row(attn_norm_w[j]), w_q[j].astype(BF16), row(q_norm_w[j]) * q_scale,
                             tm=tm, tn=TN_PROJ, n_seg_tiles=d_model // TN_PROJ)
            o = _attention(q, kv, row(lam_q1[j]), row(lam_k1[j]), row(lam_q2[j]), row(lam_k2[j]),
                           row(subln_w[j]), batch=batch, seq=seq, t=min(ATTN_TILE, seq), lambda_init=lambda_init)
            h = _matmul_res(o, w_o[j].astype(BF16), h, tm=tm, tn=TN_PROJ)
        h = _mlp(h, row(mlp_norm_w[l]), w_up[l].astype(BF16), w_down[l].astype(BF16), tm=tm, tf=TF_MLP)
    return h.reshape(batch, seq, d_model)
```

```python
import functools
import math

import jax
import jax.numpy as jnp
from jax import lax
from jax.experimental import pallas as pl
from jax.experimental.pallas import tpu as pltpu

F32 = jnp.float32
BF16 = jnp.bfloat16
EPS = 1e-5

SSM_HEAD_DIM = 64
SSM_GROUPS = 8
SSM_STATE = 128
SSM_CONV = 4
DIFF_HEAD_DIM = 128
MASK_CHUNK = 64

LANES = 128
SUBLANES = 8
VMEM_LIMIT = 56 * 1024 * 1024

NEG_BIG = -1e30
SCORE_BOUND_LOG2 = 60.0


def _params(sem):
    return pltpu.CompilerParams(dimension_semantics=sem, vmem_limit_bytes=VMEM_LIMIT)


def _rms_rows(x, w_row):
    ms = jnp.mean(x * x, axis=-1, keepdims=True)
    return x * lax.rsqrt(ms + EPS) * w_row


def _silu(x):
    h = 0.5 * x
    return h + h * jnp.tanh(h)


def _norm_matmul_kernel(x_ref, nw_ref, w_ref, sw_ref, o_ref, xn_ref, *, n_seg_tiles):
    j = pl.program_id(1)

    @pl.when(j == 0)
    def _():
        xn_ref[...] = _rms_rows(x_ref[...], nw_ref[...]).astype(BF16)

    acc = jnp.dot(xn_ref[...], w_ref[...], preferred_element_type=F32)

    if n_seg_tiles > 0:
        @pl.when(j < n_seg_tiles)
        def _():
            sw = sw_ref[...]
            for s in range(acc.shape[1] // LANES):
                blk = acc[:, s * LANES:(s + 1) * LANES]
                o_ref[:, s * LANES:(s + 1) * LANES] = _rms_rows(blk, sw).astype(o_ref.dtype)

        @pl.when(j >= n_seg_tiles)
        def _():
            o_ref[...] = acc.astype(o_ref.dtype)
    else:
        o_ref[...] = acc.astype(o_ref.dtype)


def _norm_matmul(x, nw, w, seg_w, *, tm, tn, n_seg_tiles):
    m, k = x.shape
    n = w.shape[1]
    return pl.pallas_call(
        functools.partial(_norm_matmul_kernel, n_seg_tiles=n_seg_tiles),
        grid=(m // tm, n // tn),
        in_specs=[
            pl.BlockSpec((tm, k), lambda i, j: (i, 0)),
            pl.BlockSpec((1, k), lambda i, j: (0, 0)),
            pl.BlockSpec((k, tn), lambda i, j: (0, j)),
            pl.BlockSpec((1, LANES), lambda i, j: (0, 0)),
        ],
        out_specs=pl.BlockSpec((tm, tn), lambda i, j: (i, j)),
        out_shape=jax.ShapeDtypeStruct((m, n), BF16),
        scratch_shapes=[pltpu.VMEM((tm, k), BF16)],
        compiler_params=_params(("parallel", "arbitrary")),
        name="norm_matmul",
    )(x, nw, w, seg_w)


def _in_proj_kernel(x_ref, nw_ref, w_ref, wdt_ref, dtb_ref, o_ref, dt_ref, xn_ref, *, n_heads):
    j = pl.program_id(1)

    @pl.when(j == 0)
    def _():
        xn = _rms_rows(x_ref[...], nw_ref[...]).astype(BF16)
        xn_ref[...] = xn
        pre = jnp.dot(xn, wdt_ref[...], preferred_element_type=F32)
        lane = lax.broadcasted_iota(jnp.int32, pre.shape, 1)
        pre = jnp.where(lane < n_heads, pre, 0.0) + dtb_ref[...]
        dt_ref[...] = jnp.maximum(pre, 0.0) + jnp.log1p(jnp.exp(-jnp.abs(pre)))

    o_ref[...] = jnp.dot(xn_ref[...], w_ref[...], preferred_element_type=F32).astype(o_ref.dtype)


def _in_proj(x, nw, w, dtb, *, tm, tn, n, n_heads):
    m, k = x.shape
    assert n % tn == 0 and n % LANES == 0 and n_heads <= LANES
    return pl.pallas_call(
        functools.partial(_in_proj_kernel, n_heads=n_heads),
        grid=(m // tm, n // tn),
        in_specs=[
            pl.BlockSpec((tm, k), lambda i, j: (i, 0)),
            pl.BlockSpec((1, k), lambda i, j: (0, 0)),
            pl.BlockSpec((k, tn), lambda i, j: (0, j)),
            pl.BlockSpec((k, LANES), lambda i, j: (0, n // LANES)),
            pl.BlockSpec((1, LANES), lambda i, j: (0, 0)),
        ],
        out_specs=[
            pl.BlockSpec((tm, tn), lambda i, j: (i, j)),
            pl.BlockSpec((tm, LANES), lambda i, j: (i, 0)),
        ],
        out_shape=[jax.ShapeDtypeStruct((m, n), BF16), jax.ShapeDtypeStruct((m, LANES), F32)],
        scratch_shapes=[pltpu.VMEM((tm, k), BF16)],
        compiler_params=_params(("parallel", "arbitrary")),
        name="in_proj",
    )(x, nw, w, w, dtb)


def _matmul_res_kernel(a_ref, w_ref, r_ref, o_ref):
    o_ref[...] = r_ref[...] + jnp.dot(a_ref[...], w_ref[...], preferred_element_type=F32)


def _matmul_res(a, w, res, *, tm, tn):
    m, k = a.shape
    n = w.shape[1]
    return pl.pallas_call(
        _matmul_res_kernel,
        grid=(m // tm, n // tn),
        in_specs=[
            pl.BlockSpec((tm, k), lambda i, j: (i, 0)),
            pl.BlockSpec((k, tn), lambda i, j: (0, j)),
            pl.BlockSpec((tm, tn), lambda i, j: (i, j)),
        ],
        out_specs=pl.BlockSpec((tm, tn), lambda i, j: (i, j)),
        out_shape=jax.ShapeDtypeStruct((m, n), F32),
        compiler_params=_params(("parallel", "arbitrary")),
        name="matmul_res",
    )(a, w, res)


def _mlp_kernel(x_ref, nw_ref, wu_ref, wd_ref, o_ref, xn_ref):
    j = pl.program_id(1)

    @pl.when(j == 0)
    def _():
        x = x_ref[...]
        xn_ref[...] = _rms_rows(x, nw_ref[...]).astype(BF16)
        o_ref[...] = x

    h = jnp.dot(xn_ref[...], wu_ref[...], preferred_element_type=F32)
    h = jnp.square(jnp.maximum(h, 0.0)).astype(BF16)
    o_ref[...] += jnp.dot(h, wd_ref[...], preferred_element_type=F32)


def _mlp(x, nw, wu, wd, *, tm, tf):
    m, d = x.shape
    ff = wu.shape[1]
    return pl.pallas_call(
        _mlp_kernel,
        grid=(m // tm, ff // tf),
        in_specs=[
            pl.BlockSpec((tm, d), lambda i, j: (i, 0)),
            pl.BlockSpec((1, d), lambda i, j: (0, 0)),
            pl.BlockSpec((d, tf), lambda i, j: (0, j)),
            pl.BlockSpec((tf, d), lambda i, j: (j, 0)),
        ],
        out_specs=pl.BlockSpec((tm, d), lambda i, j: (i, 0)),
        out_shape=jax.ShapeDtypeStruct((m, d), F32),
        scratch_shapes=[pltpu.VMEM((tm, d), BF16)],
        compiler_params=_params(("parallel", "arbitrary")),
        name="mlp",
    )(x, nw, wu, wd)


def _cumsum_rows(a, row):
    n = a.shape[0]
    sh = 1
    while sh < n:
        a = a + jnp.where(row >= sh, pltpu.roll(a, sh, axis=0), 0.0)
        sh *= 2
    return a


def _ssd_kernel(z_ref, xs_ref, bc_ref, dt_ref, cw_ref, cb_ref, alog_ref, dexp_ref, onw_ref, e_ref,
                o_ref, st_ref, extx_ref, extbc_ref, xc_ref, bcc_ref, ex_ref, *, L, d_inner, gn):
    G, N, P = SSM_GROUPS, SSM_STATE, SSM_HEAD_DIM
    R = d_inner // (G * P)
    GW = R * P
    T = SUBLANES
    c = pl.program_id(1)

    @pl.when(c == 0)
    def _():
        st_ref[...] = jnp.zeros(st_ref.shape, F32)
        extx_ref[0:T, :] = jnp.zeros((T, d_inner), F32)
        extbc_ref[0:T, :] = jnp.zeros((T, 2 * gn), F32)

    extx_ref[T:T + L, :] = xs_ref[...].astype(F32)
    extbc_ref[T:T + L, :] = bc_ref[...].astype(F32)
    def conv_silu(ext_ref, cols, wcols):
        e = ext_ref[:, cols]
        acc = cw_ref[0:1, wcols] * e
        for k in range(1, SSM_CONV):
            acc = pltpu.roll(acc, 1, axis=0) + cw_ref[k:k + 1, wcols] * e
        return _silu(acc[T:, :] + cb_ref[:, wcols])

    slab = 512
    for s in range(d_inner // slab):
        cols = slice(s * slab, (s + 1) * slab)
        xc_ref[:, cols] = conv_silu(extx_ref, cols, cols)
    for s in range(2 * gn // slab):
        cols = slice(s * slab, (s + 1) * slab)
        wcols = slice(d_inner + s * slab, d_inner + (s + 1) * slab)
        bcc_ref[:, cols] = conv_silu(extbc_ref, cols, wcols)
    extx_ref[0:T, :] = extx_ref[L:L + T, :]
    extbc_ref[0:T, :] = extbc_ref[L:L + T, :]

    row = lax.broadcasted_iota(jnp.int32, (L, LANES), 0)
    dt = dt_ref[...]
    da = dt * (-jnp.exp(alog_ref[...]))
    a_cs = _cumsum_rows(da, row)
    a_last = a_cs[L - 1:L, :]
    w_end = jnp.exp(a_last - a_cs) * dt
    ea = jnp.exp(a_cs)
    cd = jnp.broadcast_to(jnp.exp(a_last), (SUBLANES, LANES))
    stacked = jnp.concatenate([w_end, ea, cd], axis=0)
    hi = stacked.astype(BF16)
    lo = (stacked - hi.astype(F32)).astype(BF16)
    e = e_ref[...]
    ex_ref[...] = (jnp.dot(hi, e, preferred_element_type=F32)
                   + jnp.dot(lo, e, preferred_element_type=F32))
    a_cs_t = a_cs.T
    dt_t = dt.T

    li = lax.broadcasted_iota(jnp.int32, (L, L), 0)
    si = lax.broadcasted_iota(jnp.int32, (L, L), 1)
    tril = si <= li
    lane = lax.broadcasted_iota(jnp.int32, (L, 2 * P), 1)
    first_head = lane < P

    for g in range(G):
        gcols = slice(g * GW, (g + 1) * GW)
        b_f = bcc_ref[:, g * N:(g + 1) * N]
        c_bf = bcc_ref[:, gn + g * N:gn + (g + 1) * N].astype(BF16)
        b_t = b_f.T.astype(BF16)
        cb = jnp.dot(c_bf, b_t, preferred_element_type=F32)
        xg = xc_ref[:, gcols]
        xg_bf = xg.astype(BF16)

        st_old = st_ref[g]
        y = jnp.dot(c_bf, st_old.astype(BF16), preferred_element_type=F32) * ex_ref[L:2 * L, gcols]
        y = y + xg * dexp_ref[:, gcols]

        pairs = []
        for pr in range(R // 2):
            x_pair = xg_bf[:, pr * 2 * P:(pr + 1) * 2 * P]
            acc = None
            for q in range(2):
                h = g * R + pr * 2 + q
                seg = a_cs[:, h:h + 1] - a_cs_t[h:h + 1, :]
                decay = jnp.exp(jnp.where(tril, seg, -jnp.inf))
                m_h = (cb * decay * dt_t[h:h + 1, :]).astype(BF16)
                keep = first_head if q == 0 else jnp.logical_not(first_head)
                x_h = jnp.where(keep, x_pair, jnp.zeros_like(x_pair))
                d = jnp.dot(m_h, x_h, preferred_element_type=F32)
                acc = d if acc is None else acc + d
            pairs.append(acc)
        y = y + jnp.concatenate(pairs, axis=1)

        xw = (xg * ex_ref[0:L, gcols]).astype(BF16)
        st_ref[g] = st_old * ex_ref[2 * L:2 * L + 1, gcols] + jnp.dot(b_t, xw, preferred_element_type=F32)

        zg = z_ref[:, gcols].astype(F32)
        gt = y * _silu(zg)
        o_ref[:, gcols] = _rms_rows(gt, onw_ref[:, gcols]).astype(o_ref.dtype)


def _ssd(zx, dt, cw, cb, alog, dexp, onw, e, *, batch, seq, L, d_inner, gn):
    m = zx.shape[0]
    nc = seq // L
    nz = d_inner // (2 * gn)
    assert d_inner % (2 * gn) == 0
    kern = functools.partial(_ssd_kernel, L=L, d_inner=d_inner, gn=gn)
    row_map = lambda b, c: (b * nc + c, 0)
    const = lambda b, c: (0, 0)
    return pl.pallas_call(
        kern,
        grid=(batch, nc),
        in_specs=[
            pl.BlockSpec((L, d_inner), row_map),
            pl.BlockSpec((L, d_inner), lambda b, c: (b * nc + c, 1)),
            pl.BlockSpec((L, 2 * gn), lambda b, c: (b * nc + c, 2 * nz)),
            pl.BlockSpec((L, LANES), row_map),
            pl.BlockSpec(cw.shape, const),
            pl.BlockSpec(cb.shape, const),
            pl.BlockSpec(alog.shape, const),
            pl.BlockSpec(dexp.shape, const),
            pl.BlockSpec(onw.shape, const),
            pl.BlockSpec(e.shape, const),
        ],
        out_specs=pl.BlockSpec((L, d_inner), row_map),
        out_shape=jax.ShapeDtypeStruct((m, d_inner), BF16),
        scratch_shapes=[
            pltpu.VMEM((SSM_GROUPS, SSM_STATE, d_inner // SSM_GROUPS), F32),
            pltpu.VMEM((L + SUBLANES, d_inner), F32),
            pltpu.VMEM((L + SUBLANES, 2 * gn), F32),
            pltpu.VMEM((L, d_inner), F32),
            pltpu.VMEM((L, 2 * gn), F32),
            pltpu.VMEM((2 * L + SUBLANES, d_inner), F32),
        ],
        compiler_params=_params(("arbitrary", "arbitrary")),
        name="ssd",
    )(zx, zx, zx, dt, cw, cb, alog, dexp, onw, e)


def _attn_kernel(q_ref, k_ref, v_ref, lq1_ref, lk1_ref, lq2_ref, lk2_ref, sw_ref, o_ref,
                 sa_ref, sb_ref, p_ref, a_ref, m_ref, l_ref, acc_ref, *, t, rc, lambda_init, bounded):
    D = DIFF_HEAD_DIM
    nl = t // LANES
    qi = pl.program_id(2)
    if not bounded:
        m_ref[...] = jnp.full(m_ref.shape, NEG_BIG, F32)
    l_ref[...] = jnp.zeros(l_ref.shape, F32)
    acc_ref[...] = jnp.zeros(acc_ref.shape, F32)

    def scores(kj, s_ref):
        start = pl.multiple_of(kj * t, t)
        for c in range(2):
            k = k_ref[pl.ds(start, t), c * D:(c + 1) * D]
            s_ref[c] = lax.dot_general(q_ref[:, c * D:(c + 1) * D], k, (((1,), (1,)), ((), ())),
                                       preferred_element_type=F32)

    def softmax_pv(kj, s_ref, masked):
        start = pl.multiple_of(kj * t, t)
        v = v_ref[pl.ds(start, t), :]
        for c in range(2):
            for r in range(t // rc):
                rows = slice(r * rc, (r + 1) * rc)
                s = s_ref[c, rows, :]
                if masked:
                    qc = (lax.broadcasted_iota(jnp.int32, (rc, t), 0) + r * rc) // MASK_CHUNK
                    kc = lax.broadcasted_iota(jnp.int32, (rc, t), 1) // MASK_CHUNK
                    s = jnp.where(kc <= qc, s, NEG_BIG)
                if bounded:
                    ps = [jnp.exp2(s[:, i * LANES:(i + 1) * LANES]) for i in range(nl)]
                    psum = ps[0]
                    for i in range(1, nl):
                        psum = psum + ps[i]
                    l_ref[c, rows, :] += psum
                    p_ref[c, rows, :] = jnp.concatenate(ps, axis=1).astype(BF16)
                    continue
                m_old = m_ref[c, rows, :]
                m_new = jnp.maximum(m_old, jnp.broadcast_to(jnp.max(s, axis=1, keepdims=True), (rc, LANES)))
                alpha = jnp.exp2(m_old - m_new)
                ps = [jnp.exp2(s[:, i * LANES:(i + 1) * LANES] - m_new) for i in range(nl)]
                psum = ps[0]
                for i in range(1, nl):
                    psum = psum + ps[i]
                l_ref[c, rows, :] = alpha * l_ref[c, rows, :] + psum
                m_ref[c, rows, :] = m_new
                a_ref[c, rows, :] = alpha
                p_ref[c, rows, :] = jnp.concatenate(ps, axis=1).astype(BF16)
            pv = jnp.dot(p_ref[c], v, preferred_element_type=F32)
            if bounded:
                acc_ref[c] += pv
                continue
            alpha = a_ref[c]
            acc = acc_ref[c]
            acc_ref[c] = jnp.concatenate([acc[:, i * LANES:(i + 1) * LANES] * alpha
                                          for i in range(2 * D // LANES)], axis=1) + pv

    scores(0, sa_ref)

    def body(i, carry):
        kj = 2 * i
        scores(kj + 1, sb_ref)
        softmax_pv(kj, sa_ref, False)
        scores(kj + 2, sa_ref)
        softmax_pv(kj + 1, sb_ref, False)
        return carry

    lax.fori_loop(0, qi // 2, body, 0)

    @pl.when(qi % 2 == 0)
    def _():
        softmax_pv(qi, sa_ref, True)

    @pl.when(qi % 2 == 1)
    def _():
        scores(qi, sb_ref)
        softmax_pv(qi - 1, sa_ref, False)
        softmax_pv(qi, sb_ref, True)

    lam = (jnp.exp(jnp.sum(lq1_ref[...] * lk1_ref[...], axis=-1, keepdims=True))
           - jnp.exp(jnp.sum(lq2_ref[...] * lk2_ref[...], axis=-1, keepdims=True))
           + lambda_init)
    l0 = jnp.sum(l_ref[0], axis=-1, keepdims=True)
    l1 = jnp.sum(l_ref[1], axis=-1, keepdims=True)
    o = acc_ref[0] / l0 - lam * (acc_ref[1] / l1)
    o_ref[...] = (_rms_rows(o, sw_ref[...]) * (1.0 - lambda_init)).astype(o_ref.dtype)


def _attention(q, kv, lq1, lk1, lq2, lk2, sw, *, batch, seq, t, lambda_init, bounded):
    m, d = q.shape
    hw = 2 * DIFF_HEAD_DIM
    heads = d // hw
    nq = seq // t
    vec = pl.BlockSpec((1, DIFF_HEAD_DIM), lambda b, h, i: (0, 0))
    k_spec = pl.BlockSpec((seq, hw), lambda b, h, i: (b, h))
    v_spec = pl.BlockSpec((seq, hw), lambda b, h, i: (b, heads + h))
    return pl.pallas_call(
        functools.partial(_attn_kernel, t=t, rc=min(ATTN_ROW_CHUNK, t), lambda_init=lambda_init,
                          bounded=bounded),
        grid=(batch, heads, nq),
        in_specs=[
            pl.BlockSpec((t, hw), lambda b, h, i: (b * nq + i, h)),
            k_spec, v_spec, vec, vec, vec, vec,
            pl.BlockSpec((1, hw), lambda b, h, i: (0, 0)),
        ],
        out_specs=pl.BlockSpec((t, hw), lambda b, h, i: (b * nq + i, h)),
        out_shape=jax.ShapeDtypeStruct((m, d), BF16),
        scratch_shapes=[
            pltpu.VMEM((2, t, t), F32),
            pltpu.VMEM((2, t, t), F32),
            pltpu.VMEM((2, t, t), BF16),
            pltpu.VMEM((2, t, LANES), F32),
            pltpu.VMEM((2, t, LANES), F32),
            pltpu.VMEM((2, t, LANES), F32),
            pltpu.VMEM((2, t, hw), F32),
        ],
        compiler_params=_params(("parallel", "parallel", "arbitrary")),
        name="diff_attn",
    )(q, kv, kv, lq1, lk1, lq2, lk2, sw)


TM = 1024
TN_PROJ = 1024
TF_MLP = 512
SSD_CHUNK = 128
ATTN_TILE = 512
ATTN_ROW_CHUNK = 32


def kernel(x, mlp_norm_w, w_up, w_down, ssm_norm_w, ssm_w_in, ssm_conv_w, ssm_conv_b, ssm_dt_bias, ssm_a_log,
           ssm_d, ssm_out_norm_w, ssm_w_out, kv_norm_w, w_kv, k_norm_w, attn_norm_w, w_q, q_norm_w, lam_q1,
           lam_k1, lam_q2, lam_k2, subln_w, w_o):
    batch, seq, d_model = x.shape
    m = batch * seq
    depth = mlp_norm_w.shape[0]
    n_a = ssm_norm_w.shape[0]
    n_heads = ssm_dt_bias.shape[1]
    d_inner = n_heads * SSM_HEAD_DIM
    gn = SSM_GROUPS * SSM_STATE
    tm = min(TM, m)

    h = x.reshape(m, d_model)
    row = lambda a: a.reshape(1, -1).astype(F32)

    head_of_col = jnp.arange(d_inner, dtype=jnp.int32) // SSM_HEAD_DIM
    expand = (jnp.arange(LANES, dtype=jnp.int32)[:, None] == head_of_col[None, :]).astype(BF16)
    pad_heads = lambda a: jnp.pad(a.astype(F32), (0, LANES - n_heads)).reshape(1, LANES)

    kv = None
    for l in range(depth):
        if l < n_a:
            zx, dt = _in_proj(h, row(ssm_norm_w[l]), ssm_w_in[l].astype(BF16), pad_heads(ssm_dt_bias[l]),
                              tm=tm, tn=TN_PROJ, n=2 * d_inner + 2 * gn, n_heads=n_heads)
            y = _ssd(zx, dt, ssm_conv_w[l].astype(F32), row(ssm_conv_b[l]), pad_heads(ssm_a_log[l]),
                     row(jnp.repeat(ssm_d[l], SSM_HEAD_DIM)), row(ssm_out_norm_w[l]), expand,
                     batch=batch, seq=seq, L=min(SSD_CHUNK, seq), d_inner=d_inner, gn=gn)
            h = _matmul_res(y, ssm_w_out[l].astype(BF16), h, tm=tm, tn=TN_PROJ)
        else:
            j = l - n_a
            if j == 0:
                kv = _norm_matmul(h, row(kv_norm_w), w_kv.astype(BF16), row(k_norm_w), tm=tm, tn=TN_PROJ,
                                  n_seg_tiles=d_model // TN_PROJ)
            lambda_init = 0.8 - 0.6 * math.exp(-0.3 * l)
            q_scale = DIFF_HEAD_DIM ** -0.5 * math.log2(math.e)
            qw = row(q_norm_w[j]) * q_scale
            q = _norm_matmul(h, row(attn_norm_w[j]), w_q[j].astype(BF16), qw,
                             tm=tm, tn=TN_PROJ, n_seg_tiles=d_model // TN_PROJ)
            score_bound = (DIFF_HEAD_DIM * 1.02) * jnp.max(jnp.abs(qw)) * jnp.max(jnp.abs(k_norm_w.astype(F32)))
            attn = functools.partial(_attention, batch=batch, seq=seq, t=min(ATTN_TILE, seq),
                                     lambda_init=lambda_init)
            o = lax.cond(score_bound <= SCORE_BOUND_LOG2,
                         functools.partial(attn, bounded=True), functools.partial(attn, bounded=False),
                         q, kv, row(lam_q1[j]), row(lam_k1[j]), row(lam_q2[j]), row(lam_k2[j]), row(subln_w[j]))
            h = _matmul_res(o, w_o[j].astype(BF16), h, tm=tm, tn=TN_PROJ)
        h = _mlp(h, row(mlp_norm_w[l]), w_up[l].astype(BF16), w_down[l].astype(BF16), tm=tm, tf=TF_MLP)
    return h.reshape(batch, seq, d_model)
```

```python
import functools
import math

import jax
import jax.numpy as jnp
from jax import lax
from jax.experimental import pallas as pl
from jax.experimental.pallas import tpu as pltpu

F32 = jnp.float32
BF16 = jnp.bfloat16
EPS = 1e-5

SSM_HEAD_DIM = 64
SSM_GROUPS = 8
SSM_STATE = 128
SSM_CONV = 4
DIFF_HEAD_DIM = 128
MASK_CHUNK = 64

LANES = 128
SUBLANES = 8
VMEM_LIMIT = 60000 * 1024

NEG_BIG = -1e30
SCORE_BOUND_LOG2 = 60.0


def _params(sem):
    return pltpu.CompilerParams(dimension_semantics=sem, vmem_limit_bytes=VMEM_LIMIT)


def _rms_rows(x, w_row):
    ms = jnp.mean(x * x, axis=-1, keepdims=True)
    return x * lax.rsqrt(ms + EPS) * w_row


def _silu(x):
    h = 0.5 * x
    return h + h * jnp.tanh(h)


def _norm_matmul_kernel(x_ref, nw_ref, w_ref, sw_ref, o_ref, xn_ref, *, n_seg_tiles):
    j = pl.program_id(1)

    @pl.when(j == 0)
    def _():
        xn_ref[...] = _rms_rows(x_ref[...], nw_ref[...]).astype(BF16)

    acc = jnp.dot(xn_ref[...], w_ref[...], preferred_element_type=F32)

    if n_seg_tiles > 0:
        @pl.when(j < n_seg_tiles)
        def _():
            sw = sw_ref[...]
            for s in range(acc.shape[1] // LANES):
                blk = acc[:, s * LANES:(s + 1) * LANES]
                o_ref[:, s * LANES:(s + 1) * LANES] = _rms_rows(blk, sw).astype(o_ref.dtype)

        @pl.when(j >= n_seg_tiles)
        def _():
            o_ref[...] = acc.astype(o_ref.dtype)
    else:
        o_ref[...] = acc.astype(o_ref.dtype)


def _norm_matmul(x, nw, w, layer, seg_w, *, tm, tn, n_seg_tiles):
    m, k = x.shape
    n = w.shape[2]
    return pl.pallas_call(
        functools.partial(_norm_matmul_kernel, n_seg_tiles=n_seg_tiles),
        grid=(m // tm, n // tn),
        in_specs=[
            pl.BlockSpec((tm, k), lambda i, j: (i, 0)),
            pl.BlockSpec((1, k), lambda i, j: (0, 0)),
            pl.BlockSpec((None, k, tn), lambda i, j: (layer, 0, j)),
            pl.BlockSpec((1, LANES), lambda i, j: (0, 0)),
        ],
        out_specs=pl.BlockSpec((tm, tn), lambda i, j: (i, j)),
        out_shape=jax.ShapeDtypeStruct((m, n), BF16),
        scratch_shapes=[pltpu.VMEM((tm, k), BF16)],
        compiler_params=_params(("parallel", "arbitrary")),
        name="norm_matmul",
    )(x, nw, w, seg_w)


def _in_proj_kernel(x_ref, nw_ref, w_ref, wdt_ref, dtb_ref, o_ref, dt_ref, xn_ref, *, n_heads):
    j = pl.program_id(1)

    @pl.when(j == 0)
    def _():
        xn = _rms_rows(x_ref[...], nw_ref[...]).astype(BF16)
        xn_ref[...] = xn
        pre = jnp.dot(xn, wdt_ref[...], preferred_element_type=F32)
        lane = lax.broadcasted_iota(jnp.int32, pre.shape, 1)
        pre = jnp.where(lane < n_heads, pre, 0.0) + dtb_ref[...]
        dt_ref[...] = jnp.maximum(pre, 0.0) + jnp.log1p(jnp.exp(-jnp.abs(pre)))

    o_ref[...] = jnp.dot(xn_ref[...], w_ref[...], preferred_element_type=F32).astype(o_ref.dtype)


def _in_proj(x, nw, w, layer, dtb, *, tm, tn, n, n_heads):
    m, k = x.shape
    assert n % tn == 0 and n % LANES == 0 and n_heads <= LANES
    return pl.pallas_call(
        functools.partial(_in_proj_kernel, n_heads=n_heads),
        grid=(m // tm, n // tn),
        in_specs=[
            pl.BlockSpec((tm, k), lambda i, j: (i, 0)),
            pl.BlockSpec((1, k), lambda i, j: (0, 0)),
            pl.BlockSpec((None, k, tn), lambda i, j: (layer, 0, j)),
            pl.BlockSpec((None, k, LANES), lambda i, j: (layer, 0, n // LANES)),
            pl.BlockSpec((1, LANES), lambda i, j: (0, 0)),
        ],
        out_specs=[
            pl.BlockSpec((tm, tn), lambda i, j: (i, j)),
            pl.BlockSpec((tm, LANES), lambda i, j: (i, 0)),
        ],
        out_shape=[jax.ShapeDtypeStruct((m, n), BF16), jax.ShapeDtypeStruct((m, LANES), F32)],
        scratch_shapes=[pltpu.VMEM((tm, k), BF16)],
        compiler_params=_params(("parallel", "arbitrary")),
        name="in_proj",
    )(x, nw, w, w, dtb)


def _matmul_res_kernel(a_ref, w_ref, r_ref, o_ref):
    o_ref[...] = r_ref[...] + jnp.dot(a_ref[...], w_ref[...], preferred_element_type=F32)


def _matmul_res(a, w, layer, res, *, tm, tn):
    m, k = a.shape
    n = w.shape[2]
    return pl.pallas_call(
        _matmul_res_kernel,
        grid=(m // tm, n // tn),
        in_specs=[
            pl.BlockSpec((tm, k), lambda i, j: (i, 0)),
            pl.BlockSpec((None, k, tn), lambda i, j: (layer, 0, j)),
            pl.BlockSpec((tm, tn), lambda i, j: (i, j)),
        ],
        out_specs=pl.BlockSpec((tm, tn), lambda i, j: (i, j)),
        out_shape=jax.ShapeDtypeStruct((m, n), F32),
        compiler_params=_params(("parallel", "arbitrary")),
        name="matmul_res",
    )(a, w, res)


def _mlp_kernel(x_ref, nw_ref, wu_ref, wd_ref, o_ref, xn_ref):
    j = pl.program_id(1)

    @pl.when(j == 0)
    def _():
        x = x_ref[...]
        xn_ref[...] = _rms_rows(x, nw_ref[...]).astype(BF16)
        o_ref[...] = x

    h = jnp.dot(xn_ref[...], wu_ref[...], preferred_element_type=F32)
    h = jnp.square(jnp.maximum(h, 0.0)).astype(BF16)
    o_ref[...] += jnp.dot(h, wd_ref[...], preferred_element_type=F32)


def _mlp(x, nw, wu, wd, layer, *, tm, tf):
    m, d = x.shape
    ff = wu.shape[2]
    return pl.pallas_call(
        _mlp_kernel,
        grid=(m // tm, ff // tf),
        in_specs=[
            pl.BlockSpec((tm, d), lambda i, j: (i, 0)),
            pl.BlockSpec((1, d), lambda i, j: (0, 0)),
            pl.BlockSpec((None, d, tf), lambda i, j: (layer, 0, j)),
            pl.BlockSpec((None, tf, d), lambda i, j: (layer, j, 0)),
        ],
        out_specs=pl.BlockSpec((tm, d), lambda i, j: (i, 0)),
        out_shape=jax.ShapeDtypeStruct((m, d), F32),
        scratch_shapes=[pltpu.VMEM((tm, d), BF16)],
        compiler_params=_params(("parallel", "arbitrary")),
        name="mlp",
    )(x, nw, wu, wd)


def _cumsum_rows(a, row):
    n = a.shape[0]
    sh = 1
    while sh < n:
        a = a + jnp.where(row >= sh, pltpu.roll(a, sh, axis=0), 0.0)
        sh *= 2
    return a


def _ssd_kernel(z_ref, xs_ref, bc_ref, dt_ref, cw_ref, cb_ref, alog_ref, dexp_ref, onw_ref, e_ref,
                o_ref, st_ref, extx_ref, extbc_ref, xc_ref, bcc_ref, ex_ref, *, L, d_inner, gn):
    G, N, P = SSM_GROUPS, SSM_STATE, SSM_HEAD_DIM
    R = d_inner // (G * P)
    GW = R * P
    T = SUBLANES
    c = pl.program_id(1)

    @pl.when(c == 0)
    def _():
        st_ref[...] = jnp.zeros(st_ref.shape, F32)
        extx_ref[0:T, :] = jnp.zeros((T, d_inner), F32)
        extbc_ref[0:T, :] = jnp.zeros((T, 2 * gn), F32)

    extx_ref[T:T + L, :] = xs_ref[...].astype(F32)
    extbc_ref[T:T + L, :] = bc_ref[...].astype(F32)
    def conv_silu(ext_ref, cols, wcols):
        e = ext_ref[:, cols]
        acc = cw_ref[0:1, wcols] * e
        for k in range(1, SSM_CONV):
            acc = pltpu.roll(acc, 1, axis=0) + cw_ref[k:k + 1, wcols] * e
        return _silu(acc[T:, :] + cb_ref[:, wcols])

    slab = 512
    for s in range(d_inner // slab):
        cols = slice(s * slab, (s + 1) * slab)
        xc_ref[:, cols] = conv_silu(extx_ref, cols, cols)
    for s in range(2 * gn // slab):
        cols = slice(s * slab, (s + 1) * slab)
        wcols = slice(d_inner + s * slab, d_inner + (s + 1) * slab)
        bcc_ref[:, cols] = conv_silu(extbc_ref, cols, wcols)
    extx_ref[0:T, :] = extx_ref[L:L + T, :]
    extbc_ref[0:T, :] = extbc_ref[L:L + T, :]

    row = lax.broadcasted_iota(jnp.int32, (L, LANES), 0)
    dt = dt_ref[...]
    da = dt * (-jnp.exp(alog_ref[...]))
    a_cs = _cumsum_rows(da, row)
    a_last = a_cs[L - 1:L, :]
    w_end = jnp.exp(a_last - a_cs) * dt
    ea = jnp.exp(a_cs)
    cd = jnp.broadcast_to(jnp.exp(a_last), (SUBLANES, LANES))
    stacked = jnp.concatenate([w_end, ea, cd], axis=0)
    hi = stacked.astype(BF16)
    lo = (stacked - hi.astype(F32)).astype(BF16)
    e = e_ref[...]
    ex_ref[...] = (jnp.dot(hi, e, preferred_element_type=F32)
                   + jnp.dot(lo, e, preferred_element_type=F32))
    a_cs_t = a_cs.T
    dt_t = dt.T

    li = lax.broadcasted_iota(jnp.int32, (L, L), 0)
    si = lax.broadcasted_iota(jnp.int32, (L, L), 1)
    tril = si <= li
    lane = lax.broadcasted_iota(jnp.int32, (L, 2 * P), 1)
    first_head = lane < P

    for g in range(G):
        gcols = slice(g * GW, (g + 1) * GW)
        b_f = bcc_ref[:, g * N:(g + 1) * N]
        c_bf = bcc_ref[:, gn + g * N:gn + (g + 1) * N].astype(BF16)
        b_t = b_f.T.astype(BF16)
        cb = jnp.dot(c_bf, b_t, preferred_element_type=F32)
        xg = xc_ref[:, gcols]
        xg_bf = xg.astype(BF16)

        st_old = st_ref[g]
        y = jnp.dot(c_bf, st_old.astype(BF16), preferred_element_type=F32) * ex_ref[L:2 * L, gcols]
        y = y + xg * dexp_ref[:, gcols]

        pairs = []
        for pr in range(R // 2):
            x_pair = xg_bf[:, pr * 2 * P:(pr + 1) * 2 * P]
            acc = None
            for q in range(2):
                h = g * R + pr * 2 + q
                seg = a_cs[:, h:h + 1] - a_cs_t[h:h + 1, :]
                decay = jnp.exp(jnp.where(tril, seg, -jnp.inf))
                m_h = (cb * decay * dt_t[h:h + 1, :]).astype(BF16)
                keep = first_head if q == 0 else jnp.logical_not(first_head)
                x_h = jnp.where(keep, x_pair, jnp.zeros_like(x_pair))
                d = jnp.dot(m_h, x_h, preferred_element_type=F32)
                acc = d if acc is None else acc + d
            pairs.append(acc)
        y = y + jnp.concatenate(pairs, axis=1)

        xw = (xg * ex_ref[0:L, gcols]).astype(BF16)
        st_ref[g] = st_old * ex_ref[2 * L:2 * L + 1, gcols] + jnp.dot(b_t, xw, preferred_element_type=F32)

        zg = z_ref[:, gcols].astype(F32)
        gt = y * _silu(zg)
        o_ref[:, gcols] = _rms_rows(gt, onw_ref[:, gcols]).astype(o_ref.dtype)


def _ssd(zx, dt, cw, cb, alog, dexp, onw, e, *, batch, seq, L, d_inner, gn):
    m = zx.shape[0]
    nc = seq // L
    nz = d_inner // (2 * gn)
    assert d_inner % (2 * gn) == 0
    kern = functools.partial(_ssd_kernel, L=L, d_inner=d_inner, gn=gn)
    row_map = lambda b, c: (b * nc + c, 0)
    const = lambda b, c: (0, 0)
    return pl.pallas_call(
        kern,
        grid=(batch, nc),
        in_specs=[
            pl.BlockSpec((L, d_inner), row_map),
            pl.BlockSpec((L, d_inner), lambda b, c: (b * nc + c, 1)),
            pl.BlockSpec((L, 2 * gn), lambda b, c: (b * nc + c, 2 * nz)),
            pl.BlockSpec((L, LANES), row_map),
            pl.BlockSpec(cw.shape, const),
            pl.BlockSpec(cb.shape, const),
            pl.BlockSpec(alog.shape, const),
            pl.BlockSpec(dexp.shape, const),
            pl.BlockSpec(onw.shape, const),
            pl.BlockSpec(e.shape, const),
        ],
        out_specs=pl.BlockSpec((L, d_inner), row_map),
        out_shape=jax.ShapeDtypeStruct((m, d_inner), BF16),
        scratch_shapes=[
            pltpu.VMEM((SSM_GROUPS, SSM_STATE, d_inner // SSM_GROUPS), F32),
            pltpu.VMEM((L + SUBLANES, d_inner), F32),
            pltpu.VMEM((L + SUBLANES, 2 * gn), F32),
            pltpu.VMEM((L, d_inner), F32),
            pltpu.VMEM((L, 2 * gn), F32),
            pltpu.VMEM((2 * L + SUBLANES, d_inner), F32),
        ],
        compiler_params=_params(("arbitrary", "arbitrary")),
        name="ssd",
    )(zx, zx, zx, dt, cw, cb, alog, dexp, onw, e)


def _attn_kernel(q_ref, k_ref, v_ref, lq1_ref, lk1_ref, lq2_ref, lk2_ref, sw_ref, o_ref,
                 sa_ref, sb_ref, p_ref, a_ref, m_ref, l_ref, acc_ref, *, t, rc, lambda_init, bounded):
    D = DIFF_HEAD_DIM
    nl = t // LANES
    qi = pl.program_id(2)
    if not bounded:
        m_ref[...] = jnp.full(m_ref.shape, NEG_BIG, F32)
    l_ref[...] = jnp.zeros(l_ref.shape, F32)
    acc_ref[...] = jnp.zeros(acc_ref.shape, F32)

    def scores(kj, s_ref):
        start = pl.multiple_of(kj * t, t)
        for c in range(2):
            k = k_ref[pl.ds(start, t), c * D:(c + 1) * D]
            s_ref[c] = lax.dot_general(q_ref[:, c * D:(c + 1) * D], k, (((1,), (1,)), ((), ())),
                                       preferred_element_type=F32)

    def softmax_pv(kj, s_ref, masked):
        start = pl.multiple_of(kj * t, t)
        v = v_ref[pl.ds(start, t), :]
        for c in range(2):
            for r in range(t // rc):
                rows = slice(r * rc, (r + 1) * rc)
                vis = (r * rc // MASK_CHUNK + 1) * MASK_CHUNK if masked else t
                pieces = []
                for i in range(nl):
                    keep = min(max(vis - i * LANES, 0), LANES)
                    if keep == 0:
                        pieces.append(None)
                        continue
                    piece = s_ref[c, rows, i * LANES:(i + 1) * LANES]
                    if keep < LANES:
                        lane = lax.broadcasted_iota(jnp.int32, (rc, LANES), 1)
                        piece = jnp.where(lane < keep, piece, NEG_BIG)
                    pieces.append(piece)
                seen = [p for p in pieces if p is not None]
                if not bounded:
                    m_old = m_ref[c, rows, :]
                    m_cur = functools.reduce(jnp.maximum, seen)
                    m_new = jnp.maximum(m_old, jnp.broadcast_to(jnp.max(m_cur, axis=1, keepdims=True), (rc, LANES)))
                    alpha = jnp.exp2(m_old - m_new)
                    m_ref[c, rows, :] = m_new
                    a_ref[c, rows, :] = alpha
                    seen = [p - m_new for p in seen]
                seen = [jnp.exp2(p) for p in seen]
                psum = functools.reduce(jnp.add, seen)
                if bounded:
                    l_ref[c, rows, :] += psum
                else:
                    l_ref[c, rows, :] = alpha * l_ref[c, rows, :] + psum
                seen = iter(seen)
                ps = [next(seen) if p is not None else jnp.zeros((rc, LANES), F32) for p in pieces]
                p_ref[c, rows, :] = jnp.concatenate(ps, axis=1).astype(BF16)
            pv = jnp.dot(p_ref[c], v, preferred_element_type=F32)
            if bounded:
                acc_ref[c] += pv
                continue
            alpha = a_ref[c]
            acc = acc_ref[c]
            acc_ref[c] = jnp.concatenate([acc[:, i * LANES:(i + 1) * LANES] * alpha
                                          for i in range(2 * D // LANES)], axis=1) + pv

    scores(0, sa_ref)

    def body(i, carry):
        kj = 2 * i
        scores(kj + 1, sb_ref)
        softmax_pv(kj, sa_ref, False)
        scores(kj + 2, sa_ref)
        softmax_pv(kj + 1, sb_ref, False)
        return carry

    lax.fori_loop(0, qi // 2, body, 0)

    @pl.when(qi % 2 == 0)
    def _():
        softmax_pv(qi, sa_ref, True)

    @pl.when(qi % 2 == 1)
    def _():
        scores(qi, sb_ref)
        softmax_pv(qi - 1, sa_ref, False)
        softmax_pv(qi, sb_ref, True)

    lam = (jnp.exp(jnp.sum(lq1_ref[...] * lk1_ref[...], axis=-1, keepdims=True))
           - jnp.exp(jnp.sum(lq2_ref[...] * lk2_ref[...], axis=-1, keepdims=True))
           + lambda_init)
    l0 = jnp.sum(l_ref[0], axis=-1, keepdims=True)
    l1 = jnp.sum(l_ref[1], axis=-1, keepdims=True)
    o = acc_ref[0] / l0 - lam * (acc_ref[1] / l1)
    o_ref[...] = (_rms_rows(o, sw_ref[...]) * (1.0 - lambda_init)).astype(o_ref.dtype)


def _attention(q, kv, lq1, lk1, lq2, lk2, sw, *, batch, seq, t, lambda_init, bounded):
    m, d = q.shape
    hw = 2 * DIFF_HEAD_DIM
    heads = d // hw
    nq = seq // t
    rc = min(ATTN_ROW_CHUNK, t)
    assert MASK_CHUNK % rc == 0 and t % MASK_CHUNK == 0 and t % LANES == 0
    vec = pl.BlockSpec((1, DIFF_HEAD_DIM), lambda b, h, i: (0, 0))
    k_spec = pl.BlockSpec((seq, hw), lambda b, h, i: (b, h))
    v_spec = pl.BlockSpec((seq, hw), lambda b, h, i: (b, heads + h))
    return pl.pallas_call(
        functools.partial(_attn_kernel, t=t, rc=rc, lambda_init=lambda_init,
                          bounded=bounded),
        grid=(batch, heads, nq),
        in_specs=[
            pl.BlockSpec((t, hw), lambda b, h, i: (b * nq + i, h)),
            k_spec, v_spec, vec, vec, vec, vec,
            pl.BlockSpec((1, hw), lambda b, h, i: (0, 0)),
        ],
        out_specs=pl.BlockSpec((t, hw), lambda b, h, i: (b * nq + i, h)),
        out_shape=jax.ShapeDtypeStruct((m, d), BF16),
        scratch_shapes=[
            pltpu.VMEM((2, t, t), F32),
            pltpu.VMEM((2, t, t), F32),
            pltpu.VMEM((2, t, t), BF16),
            pltpu.VMEM((2, t, LANES), F32),
            pltpu.VMEM((2, t, LANES), F32),
            pltpu.VMEM((2, t, LANES), F32),
            pltpu.VMEM((2, t, hw), F32),
        ],
        compiler_params=_params(("parallel", "parallel", "arbitrary")),
        name="diff_attn",
    )(q, kv, kv, lq1, lk1, lq2, lk2, sw)


TM = 1024
TN_PROJ = 1024
TN_QKV = 2048
TF_MLP = 1024
SSD_CHUNK = 128
ATTN_TILE = 512
ATTN_ROW_CHUNK = 32


def kernel(x, mlp_norm_w, w_up, w_down, ssm_norm_w, ssm_w_in, ssm_conv_w, ssm_conv_b, ssm_dt_bias, ssm_a_log,
           ssm_d, ssm_out_norm_w, ssm_w_out, kv_norm_w, w_kv, k_norm_w, attn_norm_w, w_q, q_norm_w, lam_q1,
           lam_k1, lam_q2, lam_k2, subln_w, w_o):
    batch, seq, d_model = x.shape
    m = batch * seq
    depth = mlp_norm_w.shape[0]
    n_a = ssm_norm_w.shape[0]
    n_heads = ssm_dt_bias.shape[1]
    d_inner = n_heads * SSM_HEAD_DIM
    gn = SSM_GROUPS * SSM_STATE
    tm = min(TM, m)

    h = x.reshape(m, d_model)
    row = lambda a: a.reshape(1, -1).astype(F32)

    head_of_col = jnp.arange(d_inner, dtype=jnp.int32) // SSM_HEAD_DIM
    expand = (jnp.arange(LANES, dtype=jnp.int32)[:, None] == head_of_col[None, :]).astype(BF16)
    pad_heads = lambda a: jnp.pad(a.astype(F32), (0, LANES - n_heads)).reshape(1, LANES)

    w_in_b, w_out_b = ssm_w_in.astype(BF16), ssm_w_out.astype(BF16)
    w_kv_b, w_q_b, w_o_b = w_kv.astype(BF16)[None], w_q.astype(BF16), w_o.astype(BF16)
    w_up_b, w_down_b = w_up.astype(BF16), w_down.astype(BF16)

    kv = None
    for l in range(depth):
        if l < n_a:
            zx, dt = _in_proj(h, row(ssm_norm_w[l]), w_in_b, l, pad_heads(ssm_dt_bias[l]),
                              tm=tm, tn=TN_PROJ, n=2 * d_inner + 2 * gn, n_heads=n_heads)
            y = _ssd(zx, dt, ssm_conv_w[l].astype(F32), row(ssm_conv_b[l]), pad_heads(ssm_a_log[l]),
                     row(jnp.repeat(ssm_d[l], SSM_HEAD_DIM)), row(ssm_out_norm_w[l]), expand,
                     batch=batch, seq=seq, L=min(SSD_CHUNK, seq), d_inner=d_inner, gn=gn)
            h = _matmul_res(y, w_out_b, l, h, tm=tm, tn=TN_PROJ)
        else:
            j = l - n_a
            if j == 0:
                kv = _norm_matmul(h, row(kv_norm_w), w_kv_b, 0, row(k_norm_w), tm=tm, tn=TN_QKV,
                                  n_seg_tiles=d_model // TN_QKV)
            lambda_init = 0.8 - 0.6 * math.exp(-0.3 * l)
            q_scale = DIFF_HEAD_DIM ** -0.5 * math.log2(math.e)
            qw = row(q_norm_w[j]) * q_scale
            q = _norm_matmul(h, row(attn_norm_w[j]), w_q_b, j, qw,
                             tm=tm, tn=TN_QKV, n_seg_tiles=d_model // TN_QKV)
            score_bound = (DIFF_HEAD_DIM * 1.02) * jnp.max(jnp.abs(qw)) * jnp.max(jnp.abs(k_norm_w.astype(F32)))
            attn = functools.partial(_attention, batch=batch, seq=seq, t=min(ATTN_TILE, seq),
                                     lambda_init=lambda_init)
            o = lax.cond(score_bound <= SCORE_BOUND_LOG2,
                         functools.partial(attn, bounded=True), functools.partial(attn, bounded=False),
                         q, kv, row(lam_q1[j]), row(lam_k1[j]), row(lam_q2[j]), row(lam_k2[j]), row(subln_w[j]))
            h = _matmul_res(o, w_o_b, j, h, tm=tm, tn=TN_PROJ)
        h = _mlp(h, row(mlp_norm_w[l]), w_up_b, w_down_b, l, tm=tm, tf=TF_MLP)
    return h.reshape(batch, seq, d_model)
```

```python
import functools
import math

import jax
import jax.numpy as jnp
from jax import lax
from jax.experimental import pallas as pl
from jax.experimental.pallas import tpu as pltpu

F32 = jnp.float32
BF16 = jnp.bfloat16
EPS = 1e-5

SSM_HEAD_DIM = 64
SSM_GROUPS = 8
SSM_STATE = 128
SSM_CONV = 4
DIFF_HEAD_DIM = 128
MASK_CHUNK = 64

LANES = 128
SUBLANES = 8
VMEM_LIMIT = 60000 * 1024

NEG_BIG = -1e30
SCORE_BOUND_LOG2 = 60.0


def _params(sem):
    return pltpu.CompilerParams(dimension_semantics=sem, vmem_limit_bytes=VMEM_LIMIT)


def _rms_rows(x, w_row):
    ms = jnp.mean(x * x, axis=-1, keepdims=True)
    return x * lax.rsqrt(ms + EPS) * w_row


def _silu(x):
    h = 0.5 * x
    return h + h * jnp.tanh(h)


def _norm_matmul_kernel(x_ref, nw_ref, w_ref, sw_ref, o_ref, xn_ref, *, n_seg_tiles):
    j = pl.program_id(1)

    @pl.when(j == 0)
    def _():
        xn_ref[...] = _rms_rows(x_ref[...], nw_ref[...]).astype(BF16)

    acc = jnp.dot(xn_ref[...], w_ref[...], preferred_element_type=F32)

    if n_seg_tiles > 0:
        @pl.when(j < n_seg_tiles)
        def _():
            sw = sw_ref[...]
            for s in range(acc.shape[1] // LANES):
                blk = acc[:, s * LANES:(s + 1) * LANES]
                o_ref[:, s * LANES:(s + 1) * LANES] = _rms_rows(blk, sw).astype(o_ref.dtype)

        @pl.when(j >= n_seg_tiles)
        def _():
            o_ref[...] = acc.astype(o_ref.dtype)
    else:
        o_ref[...] = acc.astype(o_ref.dtype)


def _norm_matmul(x, nw, w, layer, seg_w, *, tm, tn, n_seg_tiles):
    m, k = x.shape
    n = w.shape[2]
    return pl.pallas_call(
        functools.partial(_norm_matmul_kernel, n_seg_tiles=n_seg_tiles),
        grid=(m // tm, n // tn),
        in_specs=[
            pl.BlockSpec((tm, k), lambda i, j: (i, 0)),
            pl.BlockSpec((1, k), lambda i, j: (0, 0)),
            pl.BlockSpec((None, k, tn), lambda i, j: (layer, 0, j)),
            pl.BlockSpec((1, LANES), lambda i, j: (0, 0)),
        ],
        out_specs=pl.BlockSpec((tm, tn), lambda i, j: (i, j)),
        out_shape=jax.ShapeDtypeStruct((m, n), BF16),
        scratch_shapes=[pltpu.VMEM((tm, k), BF16)],
        compiler_params=_params(("parallel", "arbitrary")),
        name="norm_matmul",
    )(x, nw, w, seg_w)


def _in_proj_kernel(x_ref, nw_ref, w_ref, wdt_ref, dtb_ref, o_ref, dt_ref, xn_ref, *, n_heads):
    j = pl.program_id(1)

    @pl.when(j == 0)
    def _():
        xn = _rms_rows(x_ref[...], nw_ref[...]).astype(BF16)
        xn_ref[...] = xn
        pre = jnp.dot(xn, wdt_ref[...], preferred_element_type=F32)
        lane = lax.broadcasted_iota(jnp.int32, pre.shape, 1)
        pre = jnp.where(lane < n_heads, pre, 0.0) + dtb_ref[...]
        dt_ref[...] = jnp.maximum(pre, 0.0) + jnp.log1p(jnp.exp(-jnp.abs(pre)))

    o_ref[...] = jnp.dot(xn_ref[...], w_ref[...], preferred_element_type=F32).astype(o_ref.dtype)


def _in_proj(x, nw, w, layer, dtb, *, tm, tn, n, n_heads):
    m, k = x.shape
    assert n % tn == 0 and n % LANES == 0 and n_heads <= LANES
    return pl.pallas_call(
        functools.partial(_in_proj_kernel, n_heads=n_heads),
        grid=(m // tm, n // tn),
        in_specs=[
            pl.BlockSpec((tm, k), lambda i, j: (i, 0)),
            pl.BlockSpec((1, k), lambda i, j: (0, 0)),
            pl.BlockSpec((None, k, tn), lambda i, j: (layer, 0, j)),
            pl.BlockSpec((None, k, LANES), lambda i, j: (layer, 0, n // LANES)),
            pl.BlockSpec((1, LANES), lambda i, j: (0, 0)),
        ],
        out_specs=[
            pl.BlockSpec((tm, tn), lambda i, j: (i, j)),
            pl.BlockSpec((tm, LANES), lambda i, j: (i, 0)),
        ],
        out_shape=[jax.ShapeDtypeStruct((m, n), BF16), jax.ShapeDtypeStruct((m, LANES), F32)],
        scratch_shapes=[pltpu.VMEM((tm, k), BF16)],
        compiler_params=_params(("parallel", "arbitrary")),
        name="in_proj",
    )(x, nw, w, w, dtb)


def _matmul_res_kernel(a_ref, w_ref, r_ref, o_ref):
    o_ref[...] = r_ref[...] + jnp.dot(a_ref[...], w_ref[...], preferred_element_type=F32)


def _matmul_res(a, w, layer, res, *, tm, tn):
    m, k = a.shape
    n = w.shape[2]
    return pl.pallas_call(
        _matmul_res_kernel,
        grid=(m // tm, n // tn),
        in_specs=[
            pl.BlockSpec((tm, k), lambda i, j: (i, 0)),
            pl.BlockSpec((None, k, tn), lambda i, j: (layer, 0, j)),
            pl.BlockSpec((tm, tn), lambda i, j: (i, j)),
        ],
        out_specs=pl.BlockSpec((tm, tn), lambda i, j: (i, j)),
        out_shape=jax.ShapeDtypeStruct((m, n), F32),
        compiler_params=_params(("parallel", "arbitrary")),
        name="matmul_res",
    )(a, w, res)


def _mlp_kernel(x_ref, nw_ref, wu_ref, wd_ref, o_ref, xn_ref):
    j = pl.program_id(1)

    @pl.when(j == 0)
    def _():
        x = x_ref[...]
        xn_ref[...] = _rms_rows(x, nw_ref[...]).astype(BF16)
        o_ref[...] = x

    h = jnp.dot(xn_ref[...], wu_ref[...], preferred_element_type=F32)
    h = jnp.square(jnp.maximum(h, 0.0)).astype(BF16)
    o_ref[...] += jnp.dot(h, wd_ref[...], preferred_element_type=F32)


def _mlp(x, nw, wu, wd, layer, *, tm, tf):
    m, d = x.shape
    ff = wu.shape[2]
    return pl.pallas_call(
        _mlp_kernel,
        grid=(m // tm, ff // tf),
        in_specs=[
            pl.BlockSpec((tm, d), lambda i, j: (i, 0)),
            pl.BlockSpec((1, d), lambda i, j: (0, 0)),
            pl.BlockSpec((None, d, tf), lambda i, j: (layer, 0, j)),
            pl.BlockSpec((None, tf, d), lambda i, j: (layer, j, 0)),
        ],
        out_specs=pl.BlockSpec((tm, d), lambda i, j: (i, 0)),
        out_shape=jax.ShapeDtypeStruct((m, d), F32),
        scratch_shapes=[pltpu.VMEM((tm, d), BF16)],
        compiler_params=_params(("parallel", "arbitrary")),
        name="mlp",
    )(x, nw, wu, wd)


def _cumsum_rows(a, row):
    n = a.shape[0]
    sh = 1
    while sh < n:
        a = a + jnp.where(row >= sh, pltpu.roll(a, sh, axis=0), 0.0)
        sh *= 2
    return a


def _ssd_kernel(z_ref, xs_ref, bc_ref, dt_ref, cw_ref, cb_ref, alog_ref, dexp_ref, onw_ref, e_ref,
                o_ref, st_ref, extx_ref, extbc_ref, xc_ref, bcc_ref, ex_ref, *, L, d_inner, gn):
    G, N, P = SSM_GROUPS, SSM_STATE, SSM_HEAD_DIM
    R = d_inner // (G * P)
    GW = R * P
    T = SUBLANES
    c = pl.program_id(1)

    @pl.when(c == 0)
    def _():
        st_ref[...] = jnp.zeros(st_ref.shape, F32)
        extx_ref[0:T, :] = jnp.zeros((T, d_inner), F32)
        extbc_ref[0:T, :] = jnp.zeros((T, 2 * gn), F32)

    extx_ref[T:T + L, :] = xs_ref[...].astype(F32)
    extbc_ref[T:T + L, :] = bc_ref[...].astype(F32)
    def conv_silu(ext_ref, cols, wcols):
        e = ext_ref[:, cols]
        acc = cw_ref[0:1, wcols] * e
        for k in range(1, SSM_CONV):
            acc = pltpu.roll(acc, 1, axis=0) + cw_ref[k:k + 1, wcols] * e
        return _silu(acc[T:, :] + cb_ref[:, wcols])

    slab = 512
    for s in range(d_inner // slab):
        cols = slice(s * slab, (s + 1) * slab)
        xc_ref[:, cols] = conv_silu(extx_ref, cols, cols)
    for s in range(2 * gn // slab):
        cols = slice(s * slab, (s + 1) * slab)
        wcols = slice(d_inner + s * slab, d_inner + (s + 1) * slab)
        bcc_ref[:, cols] = conv_silu(extbc_ref, cols, wcols)
    extx_ref[0:T, :] = extx_ref[L:L + T, :]
    extbc_ref[0:T, :] = extbc_ref[L:L + T, :]

    row = lax.broadcasted_iota(jnp.int32, (L, LANES), 0)
    dt = dt_ref[...]
    da = dt * (-jnp.exp(alog_ref[...]))
    a_cs = _cumsum_rows(da, row)
    a_last = a_cs[L - 1:L, :]
    w_end = jnp.exp(a_last - a_cs) * dt
    ea = jnp.exp(a_cs)
    cd = jnp.broadcast_to(jnp.exp(a_last), (SUBLANES, LANES))
    stacked = jnp.concatenate([w_end, ea, cd], axis=0)
    hi = stacked.astype(BF16)
    lo = (stacked - hi.astype(F32)).astype(BF16)
    e = e_ref[...]
    ex_ref[...] = (jnp.dot(hi, e, preferred_element_type=F32)
                   + jnp.dot(lo, e, preferred_element_type=F32))
    a_cs_t = a_cs.T
    dt_t = dt.T

    li = lax.broadcasted_iota(jnp.int32, (L, L), 0)
    si = lax.broadcasted_iota(jnp.int32, (L, L), 1)
    tril = si <= li
    lane = lax.broadcasted_iota(jnp.int32, (L, 2 * P), 1)
    first_head = lane < P

    for g in range(G):
        gcols = slice(g * GW, (g + 1) * GW)
        b_f = bcc_ref[:, g * N:(g + 1) * N]
        c_bf = bcc_ref[:, gn + g * N:gn + (g + 1) * N].astype(BF16)
        b_t = b_f.T.astype(BF16)
        cb = jnp.dot(c_bf, b_t, preferred_element_type=F32)
        xg = xc_ref[:, gcols]
        xg_bf = xg.astype(BF16)

        st_old = st_ref[g]
        y = jnp.dot(c_bf, st_old.astype(BF16), preferred_element_type=F32) * ex_ref[L:2 * L, gcols]
        y = y + xg * dexp_ref[:, gcols]

        pairs = []
        for pr in range(R // 2):
            x_pair = xg_bf[:, pr * 2 * P:(pr + 1) * 2 * P]
            acc = None
            for q in range(2):
                h = g * R + pr * 2 + q
                seg = a_cs[:, h:h + 1] - a_cs_t[h:h + 1, :]
                decay = jnp.exp(jnp.where(tril, seg, -jnp.inf))
                m_h = (cb * decay * dt_t[h:h + 1, :]).astype(BF16)
                keep = first_head if q == 0 else jnp.logical_not(first_head)
                x_h = jnp.where(keep, x_pair, jnp.zeros_like(x_pair))
                d = jnp.dot(m_h, x_h, preferred_element_type=F32)
                acc = d if acc is None else acc + d
            pairs.append(acc)
        y = y + jnp.concatenate(pairs, axis=1)

        xw = (xg * ex_ref[0:L, gcols]).astype(BF16)
        st_ref[g] = st_old * ex_ref[2 * L:2 * L + 1, gcols] + jnp.dot(b_t, xw, preferred_element_type=F32)

        zg = z_ref[:, gcols].astype(F32)
        gt = y * _silu(zg)
        o_ref[:, gcols] = _rms_rows(gt, onw_ref[:, gcols]).astype(o_ref.dtype)


def _ssd(zx, dt, cw, cb, alog, dexp, onw, e, *, batch, seq, L, d_inner, gn):
    m = zx.shape[0]
    nc = seq // L
    nz = d_inner // (2 * gn)
    assert d_inner % (2 * gn) == 0
    kern = functools.partial(_ssd_kernel, L=L, d_inner=d_inner, gn=gn)
    row_map = lambda b, c: (b * nc + c, 0)
    const = lambda b, c: (0, 0)
    return pl.pallas_call(
        kern,
        grid=(batch, nc),
        in_specs=[
            pl.BlockSpec((L, d_inner), row_map),
            pl.BlockSpec((L, d_inner), lambda b, c: (b * nc + c, 1)),
            pl.BlockSpec((L, 2 * gn), lambda b, c: (b * nc + c, 2 * nz)),
            pl.BlockSpec((L, LANES), row_map),
            pl.BlockSpec(cw.shape, const),
            pl.BlockSpec(cb.shape, const),
            pl.BlockSpec(alog.shape, const),
            pl.BlockSpec(dexp.shape, const),
            pl.BlockSpec(onw.shape, const),
            pl.BlockSpec(e.shape, const),
        ],
        out_specs=pl.BlockSpec((L, d_inner), row_map),
        out_shape=jax.ShapeDtypeStruct((m, d_inner), BF16),
        scratch_shapes=[
            pltpu.VMEM((SSM_GROUPS, SSM_STATE, d_inner // SSM_GROUPS), F32),
            pltpu.VMEM((L + SUBLANES, d_inner), F32),
            pltpu.VMEM((L + SUBLANES, 2 * gn), F32),
            pltpu.VMEM((L, d_inner), F32),
            pltpu.VMEM((L, 2 * gn), F32),
            pltpu.VMEM((2 * L + SUBLANES, d_inner), F32),
        ],
        compiler_params=_params(("arbitrary", "arbitrary")),
        name="ssd",
    )(zx, zx, zx, dt, cw, cb, alog, dexp, onw, e)


def _attn_kernel(q_ref, k_ref, v_ref, lq1_ref, lk1_ref, lq2_ref, lk2_ref, sw_ref, o_ref,
                 sa_ref, sb_ref, p_ref, a_ref, m_ref, l_ref, acc_ref, *, t, rc, lambda_init, bounded):
    D = DIFF_HEAD_DIM
    nl = t // LANES
    qi = pl.program_id(2)
    if not bounded:
        m_ref[...] = jnp.full(m_ref.shape, NEG_BIG, F32)
    l_ref[...] = jnp.zeros(l_ref.shape, F32)
    acc_ref[...] = jnp.zeros(acc_ref.shape, F32)

    def scores(kj, s_ref):
        start = pl.multiple_of(kj * t, t)
        for c in range(2):
            k = k_ref[pl.ds(start, t), c * D:(c + 1) * D]
            s_ref[c] = lax.dot_general(q_ref[:, c * D:(c + 1) * D], k, (((1,), (1,)), ((), ())),
                                       preferred_element_type=F32)

    def softmax_pv(kj, s_ref, masked):
        start = pl.multiple_of(kj * t, t)
        v = v_ref[pl.ds(start, t), :]
        for c in range(2):
            for r in range(t // rc):
                rows = slice(r * rc, (r + 1) * rc)
                vis = (r * rc // MASK_CHUNK + 1) * MASK_CHUNK if masked else t
                pieces = []
                for i in range(nl):
                    keep = min(max(vis - i * LANES, 0), LANES)
                    if keep == 0:
                        pieces.append(None)
                        continue
                    piece = s_ref[c, rows, i * LANES:(i + 1) * LANES]
                    if keep < LANES:
                        lane = lax.broadcasted_iota(jnp.int32, (rc, LANES), 1)
                        piece = jnp.where(lane < keep, piece, NEG_BIG)
                    pieces.append(piece)
                seen = [p for p in pieces if p is not None]
                if not bounded:
                    m_old = m_ref[c, rows, :]
                    m_cur = functools.reduce(jnp.maximum, seen)
                    m_new = jnp.maximum(m_old, jnp.broadcast_to(jnp.max(m_cur, axis=1, keepdims=True), (rc, LANES)))
                    alpha = jnp.exp2(m_old - m_new)
                    m_ref[c, rows, :] = m_new
                    a_ref[c, rows, :] = alpha
                    seen = [p - m_new for p in seen]
                seen = [jnp.exp2(p) for p in seen]
                psum = functools.reduce(jnp.add, seen)
                if bounded:
                    l_ref[c, rows, :] += psum
                else:
                    l_ref[c, rows, :] = alpha * l_ref[c, rows, :] + psum
                seen = iter(seen)
                ps = [next(seen) if p is not None else jnp.zeros((rc, LANES), F32) for p in pieces]
                p_ref[c, rows, :] = jnp.concatenate(ps, axis=1).astype(BF16)
            pv = jnp.dot(p_ref[c], v, preferred_element_type=F32)
            if bounded:
                acc_ref[c] += pv
                continue
            alpha = a_ref[c]
            acc = acc_ref[c]
            acc_ref[c] = jnp.concatenate([acc[:, i * LANES:(i + 1) * LANES] * alpha
                                          for i in range(2 * D // LANES)], axis=1) + pv

    scores(0, sa_ref)

    def body(i, carry):
        kj = 2 * i
        scores(kj + 1, sb_ref)
        softmax_pv(kj, sa_ref, False)
        scores(kj + 2, sa_ref)
        softmax_pv(kj + 1, sb_ref, False)
        return carry

    lax.fori_loop(0, qi // 2, body, 0)

    @pl.when(qi % 2 == 0)
    def _():
        softmax_pv(qi, sa_ref, True)

    @pl.when(qi % 2 == 1)
    def _():
        scores(qi, sb_ref)
        softmax_pv(qi - 1, sa_ref, False)
        softmax_pv(qi, sb_ref, True)

    lam = (jnp.exp(jnp.sum(lq1_ref[...] * lk1_ref[...], axis=-1, keepdims=True))
           - jnp.exp(jnp.sum(lq2_ref[...] * lk2_ref[...], axis=-1, keepdims=True))
           + lambda_init)
    l0 = jnp.sum(l_ref[0], axis=-1, keepdims=True)
    l1 = jnp.sum(l_ref[1], axis=-1, keepdims=True)
    o = acc_ref[0] / l0 - lam * (acc_ref[1] / l1)
    o_ref[...] = (_rms_rows(o, sw_ref[...]) * (1.0 - lambda_init)).astype(o_ref.dtype)


def _attention(q, kv, lq1, lk1, lq2, lk2, sw, *, batch, seq, t, lambda_init, bounded):
    m, d = q.shape
    hw = 2 * DIFF_HEAD_DIM
    heads = d // hw
    nq = seq // t
    rc = min(ATTN_ROW_CHUNK, t)
    assert MASK_CHUNK % rc == 0 and t % MASK_CHUNK == 0 and t % LANES == 0
    vec = pl.BlockSpec((1, DIFF_HEAD_DIM), lambda b, h, i: (0, 0))
    k_spec = pl.BlockSpec((seq, hw), lambda b, h, i: (b, h))
    v_spec = pl.BlockSpec((seq, hw), lambda b, h, i: (b, heads + h))
    return pl.pallas_call(
        functools.partial(_attn_kernel, t=t, rc=rc, lambda_init=lambda_init,
                          bounded=bounded),
        grid=(batch, heads, nq),
        in_specs=[
            pl.BlockSpec((t, hw), lambda b, h, i: (b * nq + i, h)),
            k_spec, v_spec, vec, vec, vec, vec,
            pl.BlockSpec((1, hw), lambda b, h, i: (0, 0)),
        ],
        out_specs=pl.BlockSpec((t, hw), lambda b, h, i: (b * nq + i, h)),
        out_shape=jax.ShapeDtypeStruct((m, d), BF16),
        scratch_shapes=[
            pltpu.VMEM((2, t, t), F32),
            pltpu.VMEM((2, t, t), F32),
            pltpu.VMEM((2, t, t), BF16),
            pltpu.VMEM((2, t, LANES), F32),
            pltpu.VMEM((2, t, LANES), F32),
            pltpu.VMEM((2, t, LANES), F32),
            pltpu.VMEM((2, t, hw), F32),
        ],
        compiler_params=_params(("parallel", "parallel", "arbitrary")),
        name="diff_attn",
    )(q, kv, kv, lq1, lk1, lq2, lk2, sw)


TM = 1024
TN_PROJ = 1024
TN_QKV = 2048
TF_MLP = 1024
SSD_CHUNK = 128
ATTN_TILE = 512
ATTN_ROW_CHUNK = 32


def kernel(x, mlp_norm_w, w_up, w_down, ssm_norm_w, ssm_w_in, ssm_conv_w, ssm_conv_b, ssm_dt_bias, ssm_a_log,
           ssm_d, ssm_out_norm_w, ssm_w_out, kv_norm_w, w_kv, k_norm_w, attn_norm_w, w_q, q_norm_w, lam_q1,
           lam_k1, lam_q2, lam_k2, subln_w, w_o):
    batch, seq, d_model = x.shape
    m = batch * seq
    depth = mlp_norm_w.shape[0]
    n_a = ssm_norm_w.shape[0]
    n_heads = ssm_dt_bias.shape[1]
    d_inner = n_heads * SSM_HEAD_DIM
    gn = SSM_GROUPS * SSM_STATE
    tm = min(TM, m)

    h = x.reshape(m, d_model)
    row = lambda a: a.reshape(1, -1).astype(F32)

    head_of_col = jnp.arange(d_inner, dtype=jnp.int32) // SSM_HEAD_DIM
    expand = (jnp.arange(LANES, dtype=jnp.int32)[:, None] == head_of_col[None, :]).astype(BF16)
    pad_heads = lambda a: jnp.pad(a.astype(F32), (0, LANES - n_heads)).reshape(1, LANES)

    w_in_b, w_out_b = ssm_w_in.astype(BF16), ssm_w_out.astype(BF16)
    w_kv_b, w_q_b, w_o_b = w_kv.astype(BF16)[None], w_q.astype(BF16), w_o.astype(BF16)
    w_up_b, w_down_b = w_up.astype(BF16), w_down.astype(BF16)

    kv = None
    for l in range(depth):
        if l < n_a:
            zx, dt = _in_proj(h, row(ssm_norm_w[l]), w_in_b, l, pad_heads(ssm_dt_bias[l]),
                              tm=tm, tn=TN_QKV, n=2 * d_inner + 2 * gn, n_heads=n_heads)
            y = _ssd(zx, dt, ssm_conv_w[l].astype(F32), row(ssm_conv_b[l]), pad_heads(ssm_a_log[l]),
                     row(jnp.repeat(ssm_d[l], SSM_HEAD_DIM)), row(ssm_out_norm_w[l]), expand,
                     batch=batch, seq=seq, L=min(SSD_CHUNK, seq), d_inner=d_inner, gn=gn)
            h = _matmul_res(y, w_out_b, l, h, tm=tm, tn=TN_PROJ)
        else:
            j = l - n_a
            if j == 0:
                kv = _norm_matmul(h, row(kv_norm_w), w_kv_b, 0, row(k_norm_w), tm=tm, tn=TN_QKV,
                                  n_seg_tiles=d_model // TN_QKV)
            lambda_init = 0.8 - 0.6 * math.exp(-0.3 * l)
            q_scale = DIFF_HEAD_DIM ** -0.5 * math.log2(math.e)
            qw = row(q_norm_w[j]) * q_scale
            q = _norm_matmul(h, row(attn_norm_w[j]), w_q_b, j, qw,
                             tm=tm, tn=TN_QKV, n_seg_tiles=d_model // TN_QKV)
            score_bound = (DIFF_HEAD_DIM * 1.02) * jnp.max(jnp.abs(qw)) * jnp.max(jnp.abs(k_norm_w.astype(F32)))
            attn = functools.partial(_attention, batch=batch, seq=seq, t=min(ATTN_TILE, seq),
                                     lambda_init=lambda_init)
            o = lax.cond(score_bound <= SCORE_BOUND_LOG2,
                         functools.partial(attn, bounded=True), functools.partial(attn, bounded=False),
                         q, kv, row(lam_q1[j]), row(lam_k1[j]), row(lam_q2[j]), row(lam_k2[j]), row(subln_w[j]))
            h = _matmul_res(o, w_o_b, j, h, tm=tm, tn=TN_PROJ)
        h = _mlp(h, row(mlp_norm_w[l]), w_up_b, w_down_b, l, tm=tm, tf=TF_MLP)
    return h.reshape(batch, seq, d_model)
```

```python
import functools
import math

import jax
import jax.numpy as jnp
from jax import lax
from jax.experimental import pallas as pl
from jax.experimental.pallas import tpu as pltpu

F32 = jnp.float32
BF16 = jnp.bfloat16
EPS = 1e-5

SSM_HEAD_DIM = 64
SSM_GROUPS = 8
SSM_STATE = 128
SSM_CONV = 4
DIFF_HEAD_DIM = 128
MASK_CHUNK = 64

LANES = 128
SUBLANES = 8
VMEM_LIMIT = 60000 * 1024

NEG_BIG = -1e30
SCORE_BOUND_LOG2 = 60.0


def _params(sem):
    return pltpu.CompilerParams(dimension_semantics=sem, vmem_limit_bytes=VMEM_LIMIT)


def _rms_rows(x, w_row):
    ms = jnp.mean(x * x, axis=-1, keepdims=True)
    return x * lax.rsqrt(ms + EPS) * w_row


def _silu(x):
    h = 0.5 * x
    return h + h * jnp.tanh(h)


def _norm_matmul_kernel(x_ref, nw_ref, w_ref, sw_ref, o_ref, xn_ref, *, n_seg_tiles):
    j = pl.program_id(1)

    @pl.when(j == 0)
    def _():
        xn_ref[...] = _rms_rows(x_ref[...], nw_ref[...]).astype(BF16)

    acc = jnp.dot(xn_ref[...], w_ref[...], preferred_element_type=F32)

    if n_seg_tiles > 0:
        @pl.when(j < n_seg_tiles)
        def _():
            sw = sw_ref[...]
            for s in range(acc.shape[1] // LANES):
                blk = acc[:, s * LANES:(s + 1) * LANES]
                o_ref[:, s * LANES:(s + 1) * LANES] = _rms_rows(blk, sw).astype(o_ref.dtype)

        @pl.when(j >= n_seg_tiles)
        def _():
            o_ref[...] = acc.astype(o_ref.dtype)
    else:
        o_ref[...] = acc.astype(o_ref.dtype)


def _norm_matmul(x, nw, w, layer, seg_w, *, tm, tn, n_seg_tiles):
    m, k = x.shape
    n = w.shape[2]
    return pl.pallas_call(
        functools.partial(_norm_matmul_kernel, n_seg_tiles=n_seg_tiles),
        grid=(m // tm, n // tn),
        in_specs=[
            pl.BlockSpec((tm, k), lambda i, j: (i, 0)),
            pl.BlockSpec((1, k), lambda i, j: (0, 0)),
            pl.BlockSpec((None, k, tn), lambda i, j: (layer, 0, j)),
            pl.BlockSpec((1, LANES), lambda i, j: (0, 0)),
        ],
        out_specs=pl.BlockSpec((tm, tn), lambda i, j: (i, j)),
        out_shape=jax.ShapeDtypeStruct((m, n), BF16),
        scratch_shapes=[pltpu.VMEM((tm, k), BF16)],
        compiler_params=_params(("parallel", "arbitrary")),
        name="norm_matmul",
    )(x, nw, w, seg_w)


def _in_proj_kernel(x_ref, nw_ref, w_ref, wdt_ref, dtb_ref, o_ref, dt_ref, xn_ref, *, n_heads):
    j = pl.program_id(1)

    @pl.when(j == 0)
    def _():
        xn = _rms_rows(x_ref[...], nw_ref[...]).astype(BF16)
        xn_ref[...] = xn
        pre = jnp.dot(xn, wdt_ref[...], preferred_element_type=F32)
        lane = lax.broadcasted_iota(jnp.int32, pre.shape, 1)
        pre = jnp.where(lane < n_heads, pre, 0.0) + dtb_ref[...]
        dt_ref[...] = jnp.maximum(pre, 0.0) + jnp.log1p(jnp.exp(-jnp.abs(pre)))

    o_ref[...] = jnp.dot(xn_ref[...], w_ref[...], preferred_element_type=F32).astype(o_ref.dtype)


def _in_proj(x, nw, w, layer, dtb, *, tm, tn, n, n_heads):
    m, k = x.shape
    assert n % tn == 0 and n % LANES == 0 and n_heads <= LANES
    return pl.pallas_call(
        functools.partial(_in_proj_kernel, n_heads=n_heads),
        grid=(m // tm, n // tn),
        in_specs=[
            pl.BlockSpec((tm, k), lambda i, j: (i, 0)),
            pl.BlockSpec((1, k), lambda i, j: (0, 0)),
            pl.BlockSpec((None, k, tn), lambda i, j: (layer, 0, j)),
            pl.BlockSpec((None, k, LANES), lambda i, j: (layer, 0, n // LANES)),
            pl.BlockSpec((1, LANES), lambda i, j: (0, 0)),
        ],
        out_specs=[
            pl.BlockSpec((tm, tn), lambda i, j: (i, j)),
            pl.BlockSpec((tm, LANES), lambda i, j: (i, 0)),
        ],
        out_shape=[jax.ShapeDtypeStruct((m, n), BF16), jax.ShapeDtypeStruct((m, LANES), F32)],
        scratch_shapes=[pltpu.VMEM((tm, k), BF16)],
        compiler_params=_params(("parallel", "arbitrary")),
        name="in_proj",
    )(x, nw, w, w, dtb)


def _matmul_res_kernel(a_ref, w_ref, r_ref, o_ref):
    o_ref[...] = r_ref[...] + jnp.dot(a_ref[...], w_ref[...], preferred_element_type=F32)


def _matmul_res(a, w, layer, res, *, tm, tn):
    m, k = a.shape
    n = w.shape[2]
    return pl.pallas_call(
        _matmul_res_kernel,
        grid=(m // tm, n // tn),
        in_specs=[
            pl.BlockSpec((tm, k), lambda i, j: (i, 0)),
            pl.BlockSpec((None, k, tn), lambda i, j: (layer, 0, j)),
            pl.BlockSpec((tm, tn), lambda i, j: (i, j)),
        ],
        out_specs=pl.BlockSpec((tm, tn), lambda i, j: (i, j)),
        out_shape=jax.ShapeDtypeStruct((m, n), F32),
        compiler_params=_params(("parallel", "arbitrary")),
        name="matmul_res",
    )(a, w, res)


def _mlp_kernel(x_ref, nw_ref, wu_ref, wd_ref, o_ref, xn_ref):
    j = pl.program_id(1)

    @pl.when(j == 0)
    def _():
        x = x_ref[...]
        xn_ref[...] = _rms_rows(x, nw_ref[...]).astype(BF16)
        o_ref[...] = x

    h = jnp.dot(xn_ref[...], wu_ref[...], preferred_element_type=F32)
    h = jnp.square(jnp.maximum(h, 0.0)).astype(BF16)
    o_ref[...] += jnp.dot(h, wd_ref[...], preferred_element_type=F32)


def _mlp(x, nw, wu, wd, layer, *, tm, tf):
    m, d = x.shape
    ff = wu.shape[2]
    return pl.pallas_call(
        _mlp_kernel,
        grid=(m // tm, ff // tf),
        in_specs=[
            pl.BlockSpec((tm, d), lambda i, j: (i, 0)),
            pl.BlockSpec((1, d), lambda i, j: (0, 0)),
            pl.BlockSpec((None, d, tf), lambda i, j: (layer, 0, j)),
            pl.BlockSpec((None, tf, d), lambda i, j: (layer, j, 0)),
        ],
        out_specs=pl.BlockSpec((tm, d), lambda i, j: (i, 0)),
        out_shape=jax.ShapeDtypeStruct((m, d), F32),
        scratch_shapes=[pltpu.VMEM((tm, d), BF16)],
        compiler_params=_params(("parallel", "arbitrary")),
        name="mlp",
    )(x, nw, wu, wd)


def _cumsum_rows(a, row):
    n = a.shape[0]
    sh = 1
    while sh < n:
        a = a + jnp.where(row >= sh, pltpu.roll(a, sh, axis=0), 0.0)
        sh *= 2
    return a


def _ssd_kernel(z_ref, xs_ref, bc_ref, dt_ref, cw_ref, cb_ref, alog_ref, dexp_ref, onw_ref, e_ref,
                o_ref, st_ref, extx_ref, extbc_ref, xc_ref, bcc_ref, ex_ref, *, L, d_inner, gn):
    G, N, P = SSM_GROUPS, SSM_STATE, SSM_HEAD_DIM
    R = d_inner // (G * P)
    GW = R * P
    T = SUBLANES
    c = pl.program_id(1)

    @pl.when(c == 0)
    def _():
        st_ref[...] = jnp.zeros(st_ref.shape, F32)
        extx_ref[0:T, :] = jnp.zeros((T, d_inner), F32)
        extbc_ref[0:T, :] = jnp.zeros((T, 2 * gn), F32)

    extx_ref[T:T + L, :] = xs_ref[...].astype(F32)
    extbc_ref[T:T + L, :] = bc_ref[...].astype(F32)
    def conv_silu(ext_ref, cols, wcols):
        e = ext_ref[:, cols]
        acc = cw_ref[0:1, wcols] * e
        for k in range(1, SSM_CONV):
            acc = pltpu.roll(acc, 1, axis=0) + cw_ref[k:k + 1, wcols] * e
        return _silu(acc[T:, :] + cb_ref[:, wcols])

    slab = 512
    for s in range(d_inner // slab):
        cols = slice(s * slab, (s + 1) * slab)
        xc_ref[:, cols] = conv_silu(extx_ref, cols, cols)
    for s in range(2 * gn // slab):
        cols = slice(s * slab, (s + 1) * slab)
        wcols = slice(d_inner + s * slab, d_inner + (s + 1) * slab)
        bcc_ref[:, cols] = conv_silu(extbc_ref, cols, wcols)
    extx_ref[0:T, :] = extx_ref[L:L + T, :]
    extbc_ref[0:T, :] = extbc_ref[L:L + T, :]

    row = lax.broadcasted_iota(jnp.int32, (L, LANES), 0)
    dt = dt_ref[...]
    da = dt * (-jnp.exp(alog_ref[...]))
    a_cs = _cumsum_rows(da, row)
    a_last = a_cs[L - 1:L, :]
    w_end = jnp.exp(a_last - a_cs) * dt
    ea = jnp.exp(a_cs)
    cd = jnp.broadcast_to(jnp.exp(a_last), (SUBLANES, LANES))
    stacked = jnp.concatenate([w_end, ea, cd], axis=0)
    hi = stacked.astype(BF16)
    lo = (stacked - hi.astype(F32)).astype(BF16)
    e = e_ref[...]
    ex_ref[...] = (jnp.dot(hi, e, preferred_element_type=F32)
                   + jnp.dot(lo, e, preferred_element_type=F32))
    a_cs_t = a_cs.T
    dt_t = dt.T

    li = lax.broadcasted_iota(jnp.int32, (L, L), 0)
    si = lax.broadcasted_iota(jnp.int32, (L, L), 1)
    tril = si <= li
    lane = lax.broadcasted_iota(jnp.int32, (L, 2 * P), 1)
    first_head = lane < P

    for g in range(G):
        gcols = slice(g * GW, (g + 1) * GW)
        b_f = bcc_ref[:, g * N:(g + 1) * N]
        c_bf = bcc_ref[:, gn + g * N:gn + (g + 1) * N].astype(BF16)
        b_t = b_f.T.astype(BF16)
        cb = jnp.dot(c_bf, b_t, preferred_element_type=F32)
        xg = xc_ref[:, gcols]
        xg_bf = xg.astype(BF16)

        st_old = st_ref[g]
        y = jnp.dot(c_bf, st_old.astype(BF16), preferred_element_type=F32) * ex_ref[L:2 * L, gcols]
        y = y + xg * dexp_ref[:, gcols]

        pairs = []
        for pr in range(R // 2):
            x_pair = xg_bf[:, pr * 2 * P:(pr + 1) * 2 * P]
            acc = None
            for q in range(2):
                h = g * R + pr * 2 + q
                seg = a_cs[:, h:h + 1] - a_cs_t[h:h + 1, :]
                decay = jnp.exp(jnp.where(tril, seg, -jnp.inf))
                m_h = (cb * decay * dt_t[h:h + 1, :]).astype(BF16)
                keep = first_head if q == 0 else jnp.logical_not(first_head)
                x_h = jnp.where(keep, x_pair, jnp.zeros_like(x_pair))
                d = jnp.dot(m_h, x_h, preferred_element_type=F32)
                acc = d if acc is None else acc + d
            pairs.append(acc)
        y = y + jnp.concatenate(pairs, axis=1)

        xw = (xg * ex_ref[0:L, gcols]).astype(BF16)
        st_ref[g] = st_old * ex_ref[2 * L:2 * L + 1, gcols] + jnp.dot(b_t, xw, preferred_element_type=F32)

        zg = z_ref[:, gcols].astype(F32)
        gt = y * _silu(zg)
        o_ref[:, gcols] = _rms_rows(gt, onw_ref[:, gcols]).astype(o_ref.dtype)


def _ssd(zx, dt, cw, cb, alog, dexp, onw, e, *, batch, seq, L, d_inner, gn):
    m = zx.shape[0]
    nc = seq // L
    nz = d_inner // (2 * gn)
    assert d_inner % (2 * gn) == 0
    kern = functools.partial(_ssd_kernel, L=L, d_inner=d_inner, gn=gn)
    row_map = lambda b, c: (b * nc + c, 0)
    const = lambda b, c: (0, 0)
    return pl.pallas_call(
        kern,
        grid=(batch, nc),
        in_specs=[
            pl.BlockSpec((L, d_inner), row_map),
            pl.BlockSpec((L, d_inner), lambda b, c: (b * nc + c, 1)),
            pl.BlockSpec((L, 2 * gn), lambda b, c: (b * nc + c, 2 * nz)),
            pl.BlockSpec((L, LANES), row_map),
            pl.BlockSpec(cw.shape, const),
            pl.BlockSpec(cb.shape, const),
            pl.BlockSpec(alog.shape, const),
            pl.BlockSpec(dexp.shape, const),
            pl.BlockSpec(onw.shape, const),
            pl.BlockSpec(e.shape, const),
        ],
        out_specs=pl.BlockSpec((L, d_inner), row_map),
        out_shape=jax.ShapeDtypeStruct((m, d_inner), BF16),
        scratch_shapes=[
            pltpu.VMEM((SSM_GROUPS, SSM_STATE, d_inner // SSM_GROUPS), F32),
            pltpu.VMEM((L + SUBLANES, d_inner), F32),
            pltpu.VMEM((L + SUBLANES, 2 * gn), F32),
            pltpu.VMEM((L, d_inner), F32),
            pltpu.VMEM((L, 2 * gn), F32),
            pltpu.VMEM((2 * L + SUBLANES, d_inner), F32),
        ],
        compiler_params=_params(("arbitrary", "arbitrary")),
        name="ssd",
    )(zx, zx, zx, dt, cw, cb, alog, dexp, onw, e)


def _attn_kernel(q_ref, k_ref, v_ref, lq1_ref, lk1_ref, lq2_ref, lk2_ref, sw_ref, o_ref,
                 sa_ref, sb_ref, p_ref, a_ref, m_ref, l_ref, acc_ref, *, t, rc, lambda_init, bounded):
    D = DIFF_HEAD_DIM
    nl = t // LANES
    qi = pl.program_id(2)
    if not bounded:
        m_ref[...] = jnp.full(m_ref.shape, NEG_BIG, F32)
    l_ref[...] = jnp.zeros(l_ref.shape, F32)
    acc_ref[...] = jnp.zeros(acc_ref.shape, F32)

    def scores(kj, s_ref):
        start = pl.multiple_of(kj * t, t)
        for c in range(2):
            k = k_ref[pl.ds(start, t), c * D:(c + 1) * D]
            s_ref[c] = lax.dot_general(q_ref[:, c * D:(c + 1) * D], k, (((1,), (1,)), ((), ())),
                                       preferred_element_type=F32)

    def softmax_pv(kj, s_ref, masked):
        start = pl.multiple_of(kj * t, t)
        v = v_ref[pl.ds(start, t), :]
        for c in range(2):
            for r in range(t // rc):
                rows = slice(r * rc, (r + 1) * rc)
                vis = (r * rc // MASK_CHUNK + 1) * MASK_CHUNK if masked else t
                pieces = []
                for i in range(nl):
                    keep = min(max(vis - i * LANES, 0), LANES)
                    if keep == 0:
                        pieces.append(None)
                        continue
                    piece = s_ref[c, rows, i * LANES:(i + 1) * LANES]
                    if keep < LANES:
                        lane = lax.broadcasted_iota(jnp.int32, (rc, LANES), 1)
                        piece = jnp.where(lane < keep, piece, NEG_BIG)
                    pieces.append(piece)
                seen = [p for p in pieces if p is not None]
                if not bounded:
                    m_old = m_ref[c, rows, :]
                    m_cur = functools.reduce(jnp.maximum, seen)
                    m_new = jnp.maximum(m_old, jnp.broadcast_to(jnp.max(m_cur, axis=1, keepdims=True), (rc, LANES)))
                    alpha = jnp.exp2(m_old - m_new)
                    m_ref[c, rows, :] = m_new
                    a_ref[c, rows, :] = alpha
                    seen = [p - m_new for p in seen]
                seen = [jnp.exp2(p) for p in seen]
                psum = functools.reduce(jnp.add, seen)
                if bounded:
                    l_ref[c, rows, :] += psum
                else:
                    l_ref[c, rows, :] = alpha * l_ref[c, rows, :] + psum
                seen = iter(seen)
                ps = [next(seen) if p is not None else jnp.zeros((rc, LANES), F32) for p in pieces]
                p_ref[c, rows, :] = jnp.concatenate(ps, axis=1).astype(BF16)
            if bounded:
                if c == 1:
                    pv = jnp.dot(p_ref[...].reshape(2 * t, t), v, preferred_element_type=F32)
                    acc_ref[...] += pv.reshape(2, t, 2 * D)
                continue
            pv = jnp.dot(p_ref[c], v, preferred_element_type=F32)
            alpha = a_ref[c]
            acc = acc_ref[c]
            acc_ref[c] = jnp.concatenate([acc[:, i * LANES:(i + 1) * LANES] * alpha
                                          for i in range(2 * D // LANES)], axis=1) + pv

    scores(0, sa_ref)

    def body(i, carry):
        kj = 2 * i
        scores(kj + 1, sb_ref)
        softmax_pv(kj, sa_ref, False)
        scores(kj + 2, sa_ref)
        softmax_pv(kj + 1, sb_ref, False)
        return carry

    lax.fori_loop(0, qi // 2, body, 0)

    @pl.when(qi % 2 == 0)
    def _():
        softmax_pv(qi, sa_ref, True)

    @pl.when(qi % 2 == 1)
    def _():
        scores(qi, sb_ref)
        softmax_pv(qi - 1, sa_ref, False)
        softmax_pv(qi, sb_ref, True)

    lam = (jnp.exp(jnp.sum(lq1_ref[...] * lk1_ref[...], axis=-1, keepdims=True))
           - jnp.exp(jnp.sum(lq2_ref[...] * lk2_ref[...], axis=-1, keepdims=True))
           + lambda_init)
    l0 = jnp.sum(l_ref[0], axis=-1, keepdims=True)
    l1 = jnp.sum(l_ref[1], axis=-1, keepdims=True)
    o = acc_ref[0] / l0 - lam * (acc_ref[1] / l1)
    o_ref[...] = (_rms_rows(o, sw_ref[...]) * (1.0 - lambda_init)).astype(o_ref.dtype)


def _attention(q, kv, lq1, lk1, lq2, lk2, sw, *, batch, seq, t, lambda_init, bounded):
    m, d = q.shape
    hw = 2 * DIFF_HEAD_DIM
    heads = d // hw
    nq = seq // t
    rc = min(ATTN_ROW_CHUNK, t)
    assert MASK_CHUNK % rc == 0 and t % MASK_CHUNK == 0 and t % LANES == 0
    vec = pl.BlockSpec((1, DIFF_HEAD_DIM), lambda b, h, i: (0, 0))
    k_spec = pl.BlockSpec((seq, hw), lambda b, h, i: (b, h))
    v_spec = pl.BlockSpec((seq, hw), lambda b, h, i: (b, heads + h))
    return pl.pallas_call(
        functools.partial(_attn_kernel, t=t, rc=rc, lambda_init=lambda_init,
                          bounded=bounded),
        grid=(batch, heads, nq),
        in_specs=[
            pl.BlockSpec((t, hw), lambda b, h, i: (b * nq + i, h)),
            k_spec, v_spec, vec, vec, vec, vec,
            pl.BlockSpec((1, hw), lambda b, h, i: (0, 0)),
        ],
        out_specs=pl.BlockSpec((t, hw), lambda b, h, i: (b * nq + i, h)),
        out_shape=jax.ShapeDtypeStruct((m, d), BF16),
        scratch_shapes=[
            pltpu.VMEM((2, t, t), F32),
            pltpu.VMEM((2, t, t), F32),
            pltpu.VMEM((2, t, t), BF16),
            pltpu.VMEM((2, t, LANES), F32),
            pltpu.VMEM((2, t, LANES), F32),
            pltpu.VMEM((2, t, LANES), F32),
            pltpu.VMEM((2, t, hw), F32),
        ],
        compiler_params=_params(("parallel", "parallel", "arbitrary")),
        name="diff_attn",
    )(q, kv, kv, lq1, lk1, lq2, lk2, sw)


TM = 1024
TN_PROJ = 1024
TN_QKV = 2048
TF_MLP = 1024
SSD_CHUNK = 128
ATTN_TILE = 512
ATTN_ROW_CHUNK = 32


def kernel(x, mlp_norm_w, w_up, w_down, ssm_norm_w, ssm_w_in, ssm_conv_w, ssm_conv_b, ssm_dt_bias, ssm_a_log,
           ssm_d, ssm_out_norm_w, ssm_w_out, kv_norm_w, w_kv, k_norm_w, attn_norm_w, w_q, q_norm_w, lam_q1,
           lam_k1, lam_q2, lam_k2, subln_w, w_o):
    batch, seq, d_model = x.shape
    m = batch * seq
    depth = mlp_norm_w.shape[0]
    n_a = ssm_norm_w.shape[0]
    n_heads = ssm_dt_bias.shape[1]
    d_inner = n_heads * SSM_HEAD_DIM
    gn = SSM_GROUPS * SSM_STATE
    tm = min(TM, m)

    h = x.reshape(m, d_model)
    row = lambda a: a.reshape(1, -1).astype(F32)

    head_of_col = jnp.arange(d_inner, dtype=jnp.int32) // SSM_HEAD_DIM
    expand = (jnp.arange(LANES, dtype=jnp.int32)[:, None] == head_of_col[None, :]).astype(BF16)
    pad_heads = lambda a: jnp.pad(a.astype(F32), (0, LANES - n_heads)).reshape(1, LANES)

    w_in_b, w_out_b = ssm_w_in.astype(BF16), ssm_w_out.astype(BF16)
    w_kv_b, w_q_b, w_o_b = w_kv.astype(BF16)[None], w_q.astype(BF16), w_o.astype(BF16)
    w_up_b, w_down_b = w_up.astype(BF16), w_down.astype(BF16)

    kv = None
    for l in range(depth):
        if l < n_a:
            zx, dt = _in_proj(h, row(ssm_norm_w[l]), w_in_b, l, pad_heads(ssm_dt_bias[l]),
                              tm=tm, tn=TN_QKV, n=2 * d_inner + 2 * gn, n_heads=n_heads)
            y = _ssd(zx, dt, ssm_conv_w[l].astype(F32), row(ssm_conv_b[l]), pad_heads(ssm_a_log[l]),
                     row(jnp.repeat(ssm_d[l], SSM_HEAD_DIM)), row(ssm_out_norm_w[l]), expand,
                     batch=batch, seq=seq, L=min(SSD_CHUNK, seq), d_inner=d_inner, gn=gn)
            h = _matmul_res(y, w_out_b, l, h, tm=tm, tn=TN_PROJ)
        else:
            j = l - n_a
            if j == 0:
                kv = _norm_matmul(h, row(kv_norm_w), w_kv_b, 0, row(k_norm_w), tm=tm, tn=TN_QKV,
                                  n_seg_tiles=d_model // TN_QKV)
            lambda_init = 0.8 - 0.6 * math.exp(-0.3 * l)
            q_scale = DIFF_HEAD_DIM ** -0.5 * math.log2(math.e)
            qw = row(q_norm_w[j]) * q_scale
            q = _norm_matmul(h, row(attn_norm_w[j]), w_q_b, j, qw,
                             tm=tm, tn=TN_QKV, n_seg_tiles=d_model // TN_QKV)
            score_bound = (DIFF_HEAD_DIM * 1.02) * jnp.max(jnp.abs(qw)) * jnp.max(jnp.abs(k_norm_w.astype(F32)))
            attn = functools.partial(_attention, batch=batch, seq=seq, t=min(ATTN_TILE, seq),
                                     lambda_init=lambda_init)
            o = lax.cond(score_bound <= SCORE_BOUND_LOG2,
                         functools.partial(attn, bounded=True), functools.partial(attn, bounded=False),
                         q, kv, row(lam_q1[j]), row(lam_k1[j]), row(lam_q2[j]), row(lam_k2[j]), row(subln_w[j]))
            h = _matmul_res(o, w_o_b, j, h, tm=tm, tn=TN_PROJ)
        h = _mlp(h, row(mlp_norm_w[l]), w_up_b, w_down_b, l, tm=tm, tf=TF_MLP)
    return h.reshape(batch, seq, d_model)
```

```python
import functools
import math

import jax
import jax.numpy as jnp
from jax import lax
from jax.experimental import pallas as pl
from jax.experimental.pallas import tpu as pltpu

F32 = jnp.float32
BF16 = jnp.bfloat16
EPS = 1e-5

SSM_HEAD_DIM = 64
SSM_GROUPS = 8
SSM_STATE = 128
SSM_CONV = 4
DIFF_HEAD_DIM = 128
MASK_CHUNK = 64

LANES = 128
SUBLANES = 8
VMEM_LIMIT = 60000 * 1024

NEG_BIG = -1e30
SCORE_BOUND_LOG2 = 60.0


def _params(sem):
    return pltpu.CompilerParams(dimension_semantics=sem, vmem_limit_bytes=VMEM_LIMIT)


def _rms_rows(x, w_row):
    ms = jnp.mean(x * x, axis=-1, keepdims=True)
    return x * lax.rsqrt(ms + EPS) * w_row


def _silu(x):
    h = 0.5 * x
    return h + h * jnp.tanh(h)


def _norm_matmul_kernel(x_ref, nw_ref, w_ref, sw_ref, o_ref, xn_ref, *, n_seg_tiles):
    j = pl.program_id(1)

    @pl.when(j == 0)
    def _():
        xn_ref[...] = _rms_rows(x_ref[...], nw_ref[...]).astype(BF16)

    acc = jnp.dot(xn_ref[...], w_ref[...], preferred_element_type=F32)

    if n_seg_tiles > 0:
        @pl.when(j < n_seg_tiles)
        def _():
            sw = sw_ref[...]
            for s in range(acc.shape[1] // LANES):
                blk = acc[:, s * LANES:(s + 1) * LANES]
                o_ref[:, s * LANES:(s + 1) * LANES] = _rms_rows(blk, sw).astype(o_ref.dtype)

        @pl.when(j >= n_seg_tiles)
        def _():
            o_ref[...] = acc.astype(o_ref.dtype)
    else:
        o_ref[...] = acc.astype(o_ref.dtype)


def _norm_matmul(x, nw, w, layer, seg_w, *, tm, tn, n_seg_tiles):
    m, k = x.shape
    n = w.shape[2]
    return pl.pallas_call(
        functools.partial(_norm_matmul_kernel, n_seg_tiles=n_seg_tiles),
        grid=(m // tm, n // tn),
        in_specs=[
            pl.BlockSpec((tm, k), lambda i, j: (i, 0)),
            pl.BlockSpec((1, k), lambda i, j: (0, 0)),
            pl.BlockSpec((None, k, tn), lambda i, j: (layer, 0, j)),
            pl.BlockSpec((1, LANES), lambda i, j: (0, 0)),
        ],
        out_specs=pl.BlockSpec((tm, tn), lambda i, j: (i, j)),
        out_shape=jax.ShapeDtypeStruct((m, n), BF16),
        scratch_shapes=[pltpu.VMEM((tm, k), BF16)],
        compiler_params=_params(("parallel", "arbitrary")),
        name="norm_matmul",
    )(x, nw, w, seg_w)


def _in_proj_kernel(x_ref, nw_ref, w_ref, wdt_ref, dtb_ref, o_ref, dt_ref, xn_ref, *, n_heads):
    j = pl.program_id(1)

    @pl.when(j == 0)
    def _():
        xn = _rms_rows(x_ref[...], nw_ref[...]).astype(BF16)
        xn_ref[...] = xn
        pre = jnp.dot(xn, wdt_ref[...], preferred_element_type=F32)
        lane = lax.broadcasted_iota(jnp.int32, pre.shape, 1)
        pre = jnp.where(lane < n_heads, pre, 0.0) + dtb_ref[...]
        dt_ref[...] = jnp.maximum(pre, 0.0) + jnp.log1p(jnp.exp(-jnp.abs(pre)))

    o_ref[...] = jnp.dot(xn_ref[...], w_ref[...], preferred_element_type=F32).astype(o_ref.dtype)


def _in_proj(x, nw, w, layer, dtb, *, tm, tn, n, n_heads):
    m, k = x.shape
    assert n % tn == 0 and n % LANES == 0 and n_heads <= LANES
    return pl.pallas_call(
        functools.partial(_in_proj_kernel, n_heads=n_heads),
        grid=(m // tm, n // tn),
        in_specs=[
            pl.BlockSpec((tm, k), lambda i, j: (i, 0)),
            pl.BlockSpec((1, k), lambda i, j: (0, 0)),
            pl.BlockSpec((None, k, tn), lambda i, j: (layer, 0, j)),
            pl.BlockSpec((None, k, LANES), lambda i, j: (layer, 0, n // LANES)),
            pl.BlockSpec((1, LANES), lambda i, j: (0, 0)),
        ],
        out_specs=[
            pl.BlockSpec((tm, tn), lambda i, j: (i, j)),
            pl.BlockSpec((tm, LANES), lambda i, j: (i, 0)),
        ],
        out_shape=[jax.ShapeDtypeStruct((m, n), BF16), jax.ShapeDtypeStruct((m, LANES), F32)],
        scratch_shapes=[pltpu.VMEM((tm, k), BF16)],
        compiler_params=_params(("parallel", "arbitrary")),
        name="in_proj",
    )(x, nw, w, w, dtb)


def _matmul_res_kernel(a_ref, w_ref, r_ref, o_ref):
    o_ref[...] = r_ref[...] + jnp.dot(a_ref[...], w_ref[...], preferred_element_type=F32)


def _matmul_res(a, w, layer, res, *, tm, tn):
    m, k = a.shape
    n = w.shape[2]
    return pl.pallas_call(
        _matmul_res_kernel,
        grid=(m // tm, n // tn),
        in_specs=[
            pl.BlockSpec((tm, k), lambda i, j: (i, 0)),
            pl.BlockSpec((None, k, tn), lambda i, j: (layer, 0, j)),
            pl.BlockSpec((tm, tn), lambda i, j: (i, j)),
        ],
        out_specs=pl.BlockSpec((tm, tn), lambda i, j: (i, j)),
        out_shape=jax.ShapeDtypeStruct((m, n), F32),
        compiler_params=_params(("parallel", "arbitrary")),
        name="matmul_res",
    )(a, w, res)


def _mlp_kernel(x_ref, nw_ref, wu_ref, wd_ref, o_ref, xn_ref):
    j = pl.program_id(1)

    @pl.when(j == 0)
    def _():
        x = x_ref[...]
        xn_ref[...] = _rms_rows(x, nw_ref[...]).astype(BF16)
        o_ref[...] = x

    h = jnp.dot(xn_ref[...], wu_ref[...], preferred_element_type=F32)
    h = jnp.square(jnp.maximum(h, 0.0)).astype(BF16)
    o_ref[...] += jnp.dot(h, wd_ref[...], preferred_element_type=F32)


def _mlp(x, nw, wu, wd, layer, *, tm, tf):
    m, d = x.shape
    ff = wu.shape[2]
    return pl.pallas_call(
        _mlp_kernel,
        grid=(m // tm, ff // tf),
        in_specs=[
            pl.BlockSpec((tm, d), lambda i, j: (i, 0)),
            pl.BlockSpec((1, d), lambda i, j: (0, 0)),
            pl.BlockSpec((None, d, tf), lambda i, j: (layer, 0, j)),
            pl.BlockSpec((None, tf, d), lambda i, j: (layer, j, 0)),
        ],
        out_specs=pl.BlockSpec((tm, d), lambda i, j: (i, 0)),
        out_shape=jax.ShapeDtypeStruct((m, d), F32),
        scratch_shapes=[pltpu.VMEM((tm, d), BF16)],
        compiler_params=_params(("parallel", "arbitrary")),
        name="mlp",
    )(x, nw, wu, wd)


def _cumsum_rows(a, row):
    n = a.shape[0]
    sh = 1
    while sh < n:
        a = a + jnp.where(row >= sh, pltpu.roll(a, sh, axis=0), 0.0)
        sh *= 2
    return a


def _ssd_kernel(z_ref, xs_ref, bc_ref, dt_ref, cw_ref, cb_ref, alog_ref, dexp_ref, onw_ref, e_ref,
                o_ref, st_ref, extx_ref, extbc_ref, xc_ref, bcc_ref, ex_ref, *, L, d_inner, gn):
    G, N, P = SSM_GROUPS, SSM_STATE, SSM_HEAD_DIM
    R = d_inner // (G * P)
    GW = R * P
    T = SUBLANES
    c = pl.program_id(1)

    @pl.when(c == 0)
    def _():
        st_ref[...] = jnp.zeros(st_ref.shape, F32)
        extx_ref[0:T, :] = jnp.zeros((T, d_inner), F32)
        extbc_ref[0:T, :] = jnp.zeros((T, 2 * gn), F32)

    extx_ref[T:T + L, :] = xs_ref[...].astype(F32)
    extbc_ref[T:T + L, :] = bc_ref[...].astype(F32)
    def conv_silu(ext_ref, cols, wcols):
        assert SSM_CONV == 4
        e = ext_ref[:, cols]
        e1 = pltpu.roll(e, 1, axis=0)
        early = cw_ref[0:1, wcols] * e1 + cw_ref[1:2, wcols] * e
        late = cw_ref[2:3, wcols] * e1 + cw_ref[3:4, wcols] * e
        half = (pltpu.roll(early, 2, axis=0) + late)[T:, :] + cb_ref[:, wcols]
        return half + half * jnp.tanh(half)

    slab = 512
    for s in range(d_inner // slab):
        cols = slice(s * slab, (s + 1) * slab)
        xc_ref[:, cols] = conv_silu(extx_ref, cols, cols)
    for s in range(2 * gn // slab):
        cols = slice(s * slab, (s + 1) * slab)
        wcols = slice(d_inner + s * slab, d_inner + (s + 1) * slab)
        bcc_ref[:, cols] = conv_silu(extbc_ref, cols, wcols)
    extx_ref[0:T, :] = extx_ref[L:L + T, :]
    extbc_ref[0:T, :] = extbc_ref[L:L + T, :]

    row = lax.broadcasted_iota(jnp.int32, (L, LANES), 0)
    dt = dt_ref[...]
    da = dt * (-jnp.exp(alog_ref[...]))
    a_cs = _cumsum_rows(da, row)
    a_last = a_cs[L - 1:L, :]
    w_end = jnp.exp(a_last - a_cs) * dt
    ea = jnp.exp(a_cs)
    cd = jnp.broadcast_to(jnp.exp(a_last), (SUBLANES, LANES))
    stacked = jnp.concatenate([w_end, ea, cd], axis=0)
    hi = stacked.astype(BF16)
    lo = (stacked - hi.astype(F32)).astype(BF16)
    e = e_ref[...]
    ex_ref[...] = (jnp.dot(hi, e, preferred_element_type=F32)
                   + jnp.dot(lo, e, preferred_element_type=F32))
    a_col = a_cs * math.log2(math.e)
    a_row_t = (a_col - jnp.log2(dt)).T

    li = lax.broadcasted_iota(jnp.int32, (L, L), 0)
    si = lax.broadcasted_iota(jnp.int32, (L, L), 1)
    tril = si <= li
    lane = lax.broadcasted_iota(jnp.int32, (L, 2 * P), 1)
    first_head = lane < P

    for g in range(G):
        gcols = slice(g * GW, (g + 1) * GW)
        b_f = bcc_ref[:, g * N:(g + 1) * N]
        c_bf = bcc_ref[:, gn + g * N:gn + (g + 1) * N].astype(BF16)
        b_t = b_f.T.astype(BF16)
        cb = jnp.dot(c_bf, b_t, preferred_element_type=F32)
        xg = xc_ref[:, gcols]
        xg_bf = xg.astype(BF16)

        st_old = st_ref[g]
        y = jnp.dot(c_bf, st_old.astype(BF16), preferred_element_type=F32) * ex_ref[L:2 * L, gcols]
        y = y + xg * dexp_ref[:, gcols]

        pairs = []
        for pr in range(R // 2):
            x_pair = xg_bf[:, pr * 2 * P:(pr + 1) * 2 * P]
            acc = None
            for q in range(2):
                h = g * R + pr * 2 + q
                seg = a_col[:, h:h + 1] - a_row_t[h:h + 1, :]
                decay_dt = jnp.exp2(jnp.where(tril, seg, -jnp.inf))
                m_h = (cb * decay_dt).astype(BF16)
                keep = first_head if q == 0 else jnp.logical_not(first_head)
                x_h = jnp.where(keep, x_pair, jnp.zeros_like(x_pair))
                d = jnp.dot(m_h, x_h, preferred_element_type=F32)
                acc = d if acc is None else acc + d
            pairs.append(acc)
        y = y + jnp.concatenate(pairs, axis=1)

        xw = (xg * ex_ref[0:L, gcols]).astype(BF16)
        st_ref[g] = st_old * ex_ref[2 * L:2 * L + 1, gcols] + jnp.dot(b_t, xw, preferred_element_type=F32)

        zg = z_ref[:, gcols].astype(F32)
        gt = y * _silu(zg)
        o_ref[:, gcols] = _rms_rows(gt, onw_ref[:, gcols]).astype(o_ref.dtype)


def _ssd(zx, dt, cw, cb, alog, dexp, onw, e, *, batch, seq, L, d_inner, gn):
    m = zx.shape[0]
    nc = seq // L
    nz = d_inner // (2 * gn)
    assert d_inner % (2 * gn) == 0
    kern = functools.partial(_ssd_kernel, L=L, d_inner=d_inner, gn=gn)
    row_map = lambda b, c: (b * nc + c, 0)
    const = lambda b, c: (0, 0)
    return pl.pallas_call(
        kern,
        grid=(batch, nc),
        in_specs=[
            pl.BlockSpec((L, d_inner), row_map),
            pl.BlockSpec((L, d_inner), lambda b, c: (b * nc + c, 1)),
            pl.BlockSpec((L, 2 * gn), lambda b, c: (b * nc + c, 2 * nz)),
            pl.BlockSpec((L, LANES), row_map),
            pl.BlockSpec(cw.shape, const),
            pl.BlockSpec(cb.shape, const),
            pl.BlockSpec(alog.shape, const),
            pl.BlockSpec(dexp.shape, const),
            pl.BlockSpec(onw.shape, const),
            pl.BlockSpec(e.shape, const),
        ],
        out_specs=pl.BlockSpec((L, d_inner), row_map),
        out_shape=jax.ShapeDtypeStruct((m, d_inner), BF16),
        scratch_shapes=[
            pltpu.VMEM((SSM_GROUPS, SSM_STATE, d_inner // SSM_GROUPS), F32),
            pltpu.VMEM((L + SUBLANES, d_inner), F32),
            pltpu.VMEM((L + SUBLANES, 2 * gn), F32),
            pltpu.VMEM((L, d_inner), F32),
            pltpu.VMEM((L, 2 * gn), F32),
            pltpu.VMEM((2 * L + SUBLANES, d_inner), F32),
        ],
        compiler_params=_params(("arbitrary", "arbitrary")),
        name="ssd",
    )(zx, zx, zx, dt, cw, cb, alog, dexp, onw, e)


def _attn_kernel(q_ref, k_ref, v_ref, lq1_ref, lk1_ref, lq2_ref, lk2_ref, sw_ref, o_ref,
                 sa_ref, sb_ref, p_ref, a_ref, m_ref, l_ref, acc_ref, *, t, rc, lambda_init, bounded):
    D = DIFF_HEAD_DIM
    nl = t // LANES
    qi = pl.program_id(2)
    if not bounded:
        m_ref[...] = jnp.full(m_ref.shape, NEG_BIG, F32)
    l_ref[...] = jnp.zeros(l_ref.shape, F32)
    acc_ref[...] = jnp.zeros(acc_ref.shape, F32)

    def scores(kj, s_ref):
        start = pl.multiple_of(kj * t, t)
        for c in range(2):
            k = k_ref[pl.ds(start, t), c * D:(c + 1) * D]
            s_ref[c] = lax.dot_general(q_ref[:, c * D:(c + 1) * D], k, (((1,), (1,)), ((), ())),
                                       preferred_element_type=F32)

    def softmax_pv(kj, s_ref, masked):
        start = pl.multiple_of(kj * t, t)
        v = v_ref[pl.ds(start, t), :]
        for c in range(2):
            for r in range(t // rc):
                rows = slice(r * rc, (r + 1) * rc)
                vis = (r * rc // MASK_CHUNK + 1) * MASK_CHUNK if masked else t
                pieces = []
                for i in range(nl):
                    keep = min(max(vis - i * LANES, 0), LANES)
                    if keep == 0:
                        pieces.append(None)
                        continue
                    piece = s_ref[c, rows, i * LANES:(i + 1) * LANES]
                    if keep < LANES:
                        lane = lax.broadcasted_iota(jnp.int32, (rc, LANES), 1)
                        piece = jnp.where(lane < keep, piece, NEG_BIG)
                    pieces.append(piece)
                seen = [p for p in pieces if p is not None]
                if not bounded:
                    m_old = m_ref[c, rows, :]
                    m_cur = functools.reduce(jnp.maximum, seen)
                    m_new = jnp.maximum(m_old, jnp.broadcast_to(jnp.max(m_cur, axis=1, keepdims=True), (rc, LANES)))
                    alpha = jnp.exp2(m_old - m_new)
                    m_ref[c, rows, :] = m_new
                    a_ref[c, rows, :] = alpha
                    seen = [p - m_new for p in seen]
                seen = [jnp.exp2(p) for p in seen]
                psum = functools.reduce(jnp.add, seen)
                if bounded:
                    l_ref[c, rows, :] += psum
                else:
                    l_ref[c, rows, :] = alpha * l_ref[c, rows, :] + psum
                seen = iter(seen)
                ps = [next(seen) if p is not None else jnp.zeros((rc, LANES), F32) for p in pieces]
                p_ref[c, rows, :] = jnp.concatenate(ps, axis=1).astype(BF16)
            if bounded:
                if c == 1:
                    pv = jnp.dot(p_ref[...].reshape(2 * t, t), v, preferred_element_type=F32)
                    acc_ref[...] += pv.reshape(2, t, 2 * D)
                continue
            pv = jnp.dot(p_ref[c], v, preferred_element_type=F32)
            alpha = a_ref[c]
            acc = acc_ref[c]
            acc_ref[c] = jnp.concatenate([acc[:, i * LANES:(i + 1) * LANES] * alpha
                                          for i in range(2 * D // LANES)], axis=1) + pv

    scores(0, sa_ref)

    def body(i, carry):
        kj = 2 * i
        scores(kj + 1, sb_ref)
        softmax_pv(kj, sa_ref, False)
        scores(kj + 2, sa_ref)
        softmax_pv(kj + 1, sb_ref, False)
        return carry

    lax.fori_loop(0, qi // 2, body, 0)

    @pl.when(qi % 2 == 0)
    def _():
        softmax_pv(qi, sa_ref, True)

    @pl.when(qi % 2 == 1)
    def _():
        scores(qi, sb_ref)
        softmax_pv(qi - 1, sa_ref, False)
        softmax_pv(qi, sb_ref, True)

    lam = (jnp.exp(jnp.sum(lq1_ref[...] * lk1_ref[...], axis=-1, keepdims=True))
           - jnp.exp(jnp.sum(lq2_ref[...] * lk2_ref[...], axis=-1, keepdims=True))
           + lambda_init)
    l0 = jnp.sum(l_ref[0], axis=-1, keepdims=True)
    l1 = jnp.sum(l_ref[1], axis=-1, keepdims=True)
    o = acc_ref[0] / l0 - lam * (acc_ref[1] / l1)
    o_ref[...] = (_rms_rows(o, sw_ref[...]) * (1.0 - lambda_init)).astype(o_ref.dtype)


def _attention(q, kv, lq1, lk1, lq2, lk2, sw, *, batch, seq, t, lambda_init, bounded):
    m, d = q.shape
    hw = 2 * DIFF_HEAD_DIM
    heads = d // hw
    nq = seq // t
    rc = min(ATTN_ROW_CHUNK, t)
    assert MASK_CHUNK % rc == 0 and t % MASK_CHUNK == 0 and t % LANES == 0
    vec = pl.BlockSpec((1, DIFF_HEAD_DIM), lambda b, h, i: (0, 0))
    k_spec = pl.BlockSpec((seq, hw), lambda b, h, i: (b, h))
    v_spec = pl.BlockSpec((seq, hw), lambda b, h, i: (b, heads + h))
    return pl.pallas_call(
        functools.partial(_attn_kernel, t=t, rc=rc, lambda_init=lambda_init,
                          bounded=bounded),
        grid=(batch, heads, nq),
        in_specs=[
            pl.BlockSpec((t, hw), lambda b, h, i: (b * nq + i, h)),
            k_spec, v_spec, vec, vec, vec, vec,
            pl.BlockSpec((1, hw), lambda b, h, i: (0, 0)),
        ],
        out_specs=pl.BlockSpec((t, hw), lambda b, h, i: (b * nq + i, h)),
        out_shape=jax.ShapeDtypeStruct((m, d), BF16),
        scratch_shapes=[
            pltpu.VMEM((2, t, t), F32),
            pltpu.VMEM((2, t, t), F32),
            pltpu.VMEM((2, t, t), BF16),
            pltpu.VMEM((2, t, LANES), F32),
            pltpu.VMEM((2, t, LANES), F32),
            pltpu.VMEM((2, t, LANES), F32),
            pltpu.VMEM((2, t, hw), F32),
        ],
        compiler_params=_params(("parallel", "parallel", "arbitrary")),
        name="diff_attn",
    )(q, kv, kv, lq1, lk1, lq2, lk2, sw)


TM = 1024
TN_PROJ = 1024
TN_QKV = 2048
TF_MLP = 1024
SSD_CHUNK = 128
ATTN_TILE = 512
ATTN_ROW_CHUNK = 32


def kernel(x, mlp_norm_w, w_up, w_down, ssm_norm_w, ssm_w_in, ssm_conv_w, ssm_conv_b, ssm_dt_bias, ssm_a_log,
           ssm_d, ssm_out_norm_w, ssm_w_out, kv_norm_w, w_kv, k_norm_w, attn_norm_w, w_q, q_norm_w, lam_q1,
           lam_k1, lam_q2, lam_k2, subln_w, w_o):
    batch, seq, d_model = x.shape
    m = batch * seq
    depth = mlp_norm_w.shape[0]
    n_a = ssm_norm_w.shape[0]
    n_heads = ssm_dt_bias.shape[1]
    d_inner = n_heads * SSM_HEAD_DIM
    gn = SSM_GROUPS * SSM_STATE
    tm = min(TM, m)

    h = x.reshape(m, d_model)
    row = lambda a: a.reshape(1, -1).astype(F32)

    head_of_col = jnp.arange(d_inner, dtype=jnp.int32) // SSM_HEAD_DIM
    expand = (jnp.arange(LANES, dtype=jnp.int32)[:, None] == head_of_col[None, :]).astype(BF16)
    pad_heads = lambda a: jnp.pad(a.astype(F32), (0, LANES - n_heads)).reshape(1, LANES)

    w_in_b, w_out_b = ssm_w_in.astype(BF16), ssm_w_out.astype(BF16)
    w_kv_b, w_q_b, w_o_b = w_kv.astype(BF16)[None], w_q.astype(BF16), w_o.astype(BF16)
    w_up_b, w_down_b = w_up.astype(BF16), w_down.astype(BF16)

    kv = None
    for l in range(depth):
        if l < n_a:
            zx, dt = _in_proj(h, row(ssm_norm_w[l]), w_in_b, l, pad_heads(ssm_dt_bias[l]),
                              tm=tm, tn=TN_QKV, n=2 * d_inner + 2 * gn, n_heads=n_heads)
            y = _ssd(zx, dt, 0.5 * ssm_conv_w[l].astype(F32), 0.5 * row(ssm_conv_b[l]), pad_heads(ssm_a_log[l]),
                     row(jnp.repeat(ssm_d[l], SSM_HEAD_DIM)), row(ssm_out_norm_w[l]), expand,
                     batch=batch, seq=seq, L=min(SSD_CHUNK, seq), d_inner=d_inner, gn=gn)
            h = _matmul_res(y, w_out_b, l, h, tm=tm, tn=TN_PROJ)
        else:
            j = l - n_a
            if j == 0:
                kv = _norm_matmul(h, row(kv_norm_w), w_kv_b, 0, row(k_norm_w), tm=tm, tn=TN_QKV,
                                  n_seg_tiles=d_model // TN_QKV)
            lambda_init = 0.8 - 0.6 * math.exp(-0.3 * l)
            q_scale = DIFF_HEAD_DIM ** -0.5 * math.log2(math.e)
            qw = row(q_norm_w[j]) * q_scale
            q = _norm_matmul(h, row(attn_norm_w[j]), w_q_b, j, qw,
                             tm=tm, tn=TN_QKV, n_seg_tiles=d_model // TN_QKV)
            score_bound = (DIFF_HEAD_DIM * 1.02) * jnp.max(jnp.abs(qw)) * jnp.max(jnp.abs(k_norm_w.astype(F32)))
            attn = functools.partial(_attention, batch=batch, seq=seq, t=min(ATTN_TILE, seq),
                                     lambda_init=lambda_init)
            o = lax.cond(score_bound <= SCORE_BOUND_LOG2,
                         functools.partial(attn, bounded=True), functools.partial(attn, bounded=False),
                         q, kv, row(lam_q1[j]), row(lam_k1[j]), row(lam_q2[j]), row(lam_k2[j]), row(subln_w[j]))
            h = _matmul_res(o, w_o_b, j, h, tm=tm, tn=TN_PROJ)
        h = _mlp(h, row(mlp_norm_w[l]), w_up_b, w_down_b, l, tm=tm, tf=TF_MLP)
    return h.reshape(batch, seq, d_model)
```
